```python
import math, functools
import jax, jax.numpy as jnp
from jax import lax
import numpy as np

D_MODEL = 1024
BATCH = 8
SEQ = 2048
DEPTH = 4
DEC_BATCH = 32
DEC_SEQ = 1
PAST_LEN = 8192
PAGE_SIZE = 128

N_META = 16
N_EVEN = (DEPTH + 1) // 2
N_ODD = DEPTH // 2
H_A = 8
DK_A = 128
DV_A = 128
W_A = H_A * DV_A
GDN_QKV = 2 * H_A * DK_A + H_A * DV_A
GDN_CHUNK = 64
H_B = 8
DH_B = 128
W_B = H_B * DH_B
SB_BLOCK = 128
SB_BIAS_INIT = -8.0
W_C = 2 * D_MODEL
H_C = 16
BW_C = W_C // H_C
RG_C = 8.0
CONV_W = 4
EPS = 1e-6
OFF_QKV_B = GDN_QKV
OFF_Z_A = OFF_QKV_B + 3 * W_B
OFF_Z_B = OFF_Z_A + W_A
OFF_BETA = OFF_Z_B + W_B
OFF_ALPHA = OFF_BETA + H_A
EVEN_IN = OFF_ALPHA + H_A

kernel_name = "hybrid_gdn_stickbreak_rglru_step"


def rmsnorm(x, w):
    xf = x.astype(jnp.float32)
    y = xf * lax.rsqrt(jnp.mean(xf * xf, axis=-1, keepdims=True) + EPS)
    return (y * w.astype(jnp.float32)).astype(x.dtype)


def l2norm(x):
    xf = x.astype(jnp.float32)
    return xf * lax.rsqrt(jnp.sum(xf * xf, axis=-1, keepdims=True) + EPS)


def causal_conv(u, buf, w, b):
    L = u.shape[1]
    full = jnp.concatenate([buf.astype(u.dtype), u], axis=1)
    out = sum(full[:, t:t + L] * w[t].astype(u.dtype) for t in range(CONV_W))
    if b is not None:
        out = out + b.astype(u.dtype)
    return out, full[:, L:]


def gdn_chunk(S, q, k, v, beta, g):
    L = q.shape[2]
    gc = jnp.cumsum(g, axis=-1)
    idx = jnp.arange(L)
    lower = idx[:, None] >= idx[None, :]
    strict = idx[:, None] > idx[None, :]
    decay = jnp.exp(jnp.where(lower, gc[..., :, None] - gc[..., None, :], -jnp.inf))
    kb = k * beta[..., None]
    m = jnp.where(strict, jnp.einsum('bhid,bhjd->bhij', kb, k) * decay, 0.0)
    tmat = jnp.eye(L, dtype=jnp.float32) + m
    rhs = jnp.concatenate([v * beta[..., None], kb * jnp.exp(gc)[..., None]], axis=-1)
    sol = lax.linalg.triangular_solve(tmat, rhs, left_side=True, lower=True)
    u, w = sol[..., :DV_A], sol[..., DV_A:]
    v_new = u - jnp.einsum('bhld,bhde->bhle', w, S)
    qk = jnp.einsum('bhid,bhjd->bhij', q, k) * decay
    o = (jnp.einsum('bhld,bhde->bhle', q * jnp.exp(gc)[..., None], S)
         + jnp.einsum('bhij,bhje->bhie', qk, v_new))
    g_last = gc[..., -1]
    S_new = (S * jnp.exp(g_last)[..., None, None]
             + jnp.einsum('bhld,bhle->bhde', k * jnp.exp(g_last[..., None] - gc)[..., None], v_new))
    return S_new, o


def gdn_step(S, q, k, v, beta, g):
    S_new, o = gdn_chunk(S.astype(jnp.float32), q, k, v, beta, g)
    return o, S_new


def gdn_prompt(q, k, v, beta, g):
    bsz, h, L = q.shape[:3]
    S0 = jnp.zeros((bsz, h, DK_A, DV_A), jnp.float32)
    S1, o_meta = gdn_chunk(S0, q[:, :, :N_META], k[:, :, :N_META], v[:, :, :N_META],
                           beta[:, :, :N_META], g[:, :, :N_META])
    n = (L - N_META) // GDN_CHUNK

    def split(t):
        t = t[:, :, N_META:]
        return jnp.moveaxis(t.reshape(bsz, h, n, GDN_CHUNK, *t.shape[3:]), 2, 0)

    S_fin, o_rest = lax.scan(lambda S, xs: gdn_chunk(S, *xs), S1,
                             (split(q), split(k), split(v), split(beta), split(g)))
    o_rest = jnp.moveaxis(o_rest, 0, 2).reshape(bsz, h, n * GDN_CHUNK, DV_A)
    return jnp.concatenate([o_meta, o_rest], axis=2), S_fin


def gdn_inputs(qkv_conv, b_raw, a_raw, a_log, dt_bias):
    bsz, L = qkv_conv.shape[:2]
    act = jax.nn.silu(qkv_conv)
    q = act[..., :H_A * DK_A].reshape(bsz, L, H_A, DK_A)
    k = act[..., H_A * DK_A:2 * H_A * DK_A].reshape(bsz, L, H_A, DK_A)
    v = act[..., 2 * H_A * DK_A:].reshape(bsz, L, H_A, DV_A).astype(jnp.float32)
    q = l2norm(q) * (DK_A ** -0.5)
    k = l2norm(k)
    beta = jax.nn.sigmoid(b_raw.astype(jnp.float32))
    g = -jnp.exp(a_log.astype(jnp.float32)) * jax.nn.softplus(
        a_raw.astype(jnp.float32) + dt_bias.astype(jnp.float32))
    tr = lambda t: jnp.swapaxes(t, 1, 2)
    return tr(q), tr(k), tr(v), tr(beta), tr(g)


def sb_attend(q, k, v, bias, q_pos, k_pos):
    z = (jnp.einsum('bqhd,bkhd->bhqk', q.astype(jnp.float32), k.astype(jnp.float32)) * (DH_B ** -0.5)
         + bias.astype(jnp.float32)[None, :, None, None])
    vis = k_pos[None, :] < q_pos[:, None]
    log_1m = jnp.where(vis, jax.nn.log_sigmoid(-z), 0.0)
    excl = lax.cumsum(log_1m, axis=3, reverse=True) - log_1m
    a = jnp.where(vis, jnp.exp(jax.nn.log_sigmoid(z) + excl), 0.0)
    return jnp.einsum('bhqk,bkhd->bqhd', a, v.astype(jnp.float32)).astype(v.dtype)


def sb_prompt(q, k, v, bias):
    bsz, L = q.shape[:2]
    pos = jnp.arange(L)
    o_meta = sb_attend(q[:, :N_META], k[:, :N_META], v[:, :N_META], bias, pos[:N_META], pos[:N_META])
    n = (L - N_META) // SB_BLOCK
    qb = jnp.moveaxis(q[:, N_META:].reshape(bsz, n, SB_BLOCK, H_B, DH_B), 1, 0)
    pb = pos[N_META:].reshape(n, SB_BLOCK)
    o = lax.map(lambda a: sb_attend(a[0], k, v, bias, a[1], pos), (qb, pb))
    o = jnp.moveaxis(o, 0, 1).reshape(bsz, n * SB_BLOCK, H_B, DH_B)
    return jnp.concatenate([o_meta, o], axis=1)


def sb_sample(q, k, v, bias, cache_k, cache_v, page_table, layer):
    bd, ls = q.shape[:2]
    k_past = cache_k[layer][page_table].reshape(bd, PAST_LEN, H_B, DH_B).astype(k.dtype)
    v_past = cache_v[layer][page_table].reshape(bd, PAST_LEN, H_B, DH_B).astype(v.dtype)
    k_all = jnp.concatenate([k_past, k], axis=1)
    v_all = jnp.concatenate([v_past, v], axis=1)
    k_pos = jnp.arange(PAST_LEN + ls)
    q_pos = PAST_LEN + jnp.arange(ls)
    return sb_attend(q, k_all, v_all, bias, q_pos, k_pos)


def rglru(xc, h0, w_a, b_a, w_x, b_x, lam):
    bsz, L = xc.shape[:2]
    xf = xc.astype(jnp.float32)
    xb = xf.reshape(bsz, L, H_C, BW_C)
    r = jax.nn.sigmoid(jnp.einsum('blhi,hij->blhj', xb, w_a.astype(jnp.float32)).reshape(bsz, L, W_C)
                       + b_a.astype(jnp.float32))
    i = jax.nn.sigmoid(jnp.einsum('blhi,hij->blhj', xb, w_x.astype(jnp.float32)).reshape(bsz, L, W_C)
                       + b_x.astype(jnp.float32))
    log_a = RG_C * r * jax.nn.log_sigmoid(lam.astype(jnp.float32))
    a = jnp.exp(log_a)
    b = jnp.sqrt(-jnp.expm1(2.0 * log_a)) * (i * xf)
    b = b.at[:, 0].add(a[:, 0] * h0.astype(jnp.float32))

    def combine(e1, e2):
        return (e1[0] * e2[0], e2[0] * e1[1] + e2[1])

    _, h = lax.associative_scan(combine, (a, b), axis=1)
    return h


def even_layer(x, nw, w_in, conv_w, a_log, dt_bias, gn_w, sb_bias, w_out, conv_buf, gdn_fn, sb_fn):
    bsz, L = x.shape[:2]
    p = rmsnorm(x, nw) @ w_in
    qkv_a = p[..., :OFF_QKV_B]
    qkv_b = p[..., OFF_QKV_B:OFF_Z_A]
    z_a = p[..., OFF_Z_A:OFF_Z_B]
    z_b = p[..., OFF_Z_B:OFF_BETA]
    b_raw = p[..., OFF_BETA:OFF_ALPHA]
    a_raw = p[..., OFF_ALPHA:]
    qkv_a, new_buf = causal_conv(qkv_a, conv_buf, conv_w, None)
    q, k, v, beta, g = gdn_inputs(qkv_a, b_raw, a_raw, a_log, dt_bias)
    o_a, S_new = gdn_fn(q, k, v, beta, g)
    o_a = rmsnorm(jnp.swapaxes(o_a, 1, 2), gn_w).reshape(bsz, L, W_A).astype(x.dtype) * jax.nn.silu(z_a)
    q_b = qkv_b[..., :W_B].reshape(bsz, L, H_B, DH_B)
    k_b = qkv_b[..., W_B:2 * W_B].reshape(bsz, L, H_B, DH_B)
    v_b = qkv_b[..., 2 * W_B:].reshape(bsz, L, H_B, DH_B)
    o_b = sb_fn(q_b, k_b, v_b, sb_bias).reshape(bsz, L, W_B).astype(x.dtype) * jax.nn.silu(z_b)
    y = x + jnp.concatenate([o_a, o_b], axis=-1) @ w_out
    return y, new_buf, S_new, k_b, v_b


def odd_layer(x, nw, w_in, conv_w, conv_b, w_a, b_a, w_x, b_x, lam, w_out, conv_buf, h0):
    p = rmsnorm(x, nw) @ w_in
    xb, gate = p[..., :W_C], p[..., W_C:]
    xc, new_buf = causal_conv(xb, conv_buf, conv_w, conv_b)
    h = rglru(xc, h0, w_a, b_a, w_x, b_x, lam)
    y = x + (h.astype(x.dtype) * jax.nn.silu(gate)) @ w_out
    return y, new_buf, h[:, -1]


def setup_inputs(seed: int = 0) -> dict:
    key = jax.random.key(seed)
    ks = jax.random.split(key, 32)
    f32 = jnp.float32
    n_pages = PAST_LEN // PAGE_SIZE
    n_used = DEC_BATCH * n_pages
    n_phys = n_used + n_used // 4
    nrm = lambda k, shape, s: s * jax.random.normal(k, shape, f32)
    x_prompt = nrm(ks[0], (BATCH, SEQ, D_MODEL), 1.0)
    x_sample = nrm(ks[1], (DEC_BATCH, DEC_SEQ, D_MODEL), 1.0)
    cache_sb_k = nrm(ks[2], (N_EVEN, n_phys, PAGE_SIZE, H_B, DH_B), 1.0)
    cache_sb_v = nrm(ks[3], (N_EVEN, n_phys, PAGE_SIZE, H_B, DH_B), 1.0)
    state_gdn = nrm(ks[5], (N_EVEN, DEC_BATCH, H_A, DK_A, DV_A), 0.5)
    state_gdn_conv = nrm(ks[6], (N_EVEN, DEC_BATCH, CONV_W - 1, GDN_QKV), 1.0)
    state_lru = nrm(ks[7], (N_ODD, DEC_BATCH, W_C), 1.0)
    state_lru_conv = nrm(ks[8], (N_ODD, DEC_BATCH, CONV_W - 1, W_C), 1.0)
    page_table = jax.random.permutation(ks[4], n_phys)[:n_used].reshape(DEC_BATCH, n_pages).astype(jnp.int32)
    meta_tokens = nrm(ks[9], (N_META, D_MODEL), 1.0)
    norm_w = 1.0 + nrm(ks[10], (DEPTH, D_MODEL), 0.01)
    final_norm_w = 1.0 + nrm(ks[11], (D_MODEL,), 0.01)
    w_in_even = nrm(ks[12], (N_EVEN, D_MODEL, EVEN_IN), D_MODEL ** -0.5)
    gdn_conv_w = nrm(ks[13], (N_EVEN, CONV_W, GDN_QKV), CONV_W ** -0.5)
    gdn_a_log = jnp.log(jax.random.uniform(ks[14], (N_EVEN, H_A), f32, 1.0, 16.0))
    dt = jnp.exp(jax.random.uniform(ks[15], (N_EVEN, H_A), f32, math.log(1e-3), math.log(1e-1)))
    gdn_dt_bias = dt + jnp.log(-jnp.expm1(-dt))
    gdn_norm_w = 1.0 + nrm(ks[16], (N_EVEN, DV_A), 0.01)
    sb_bias = SB_BIAS_INIT + nrm(ks[27], (N_EVEN, H_B), 0.1)
    w_out_even = nrm(ks[17], (N_EVEN, W_A + W_B, D_MODEL), (W_A + W_B) ** -0.5)
    w_in_odd = nrm(ks[18], (N_ODD, D_MODEL, 2 * W_C), D_MODEL ** -0.5)
    lru_conv_w = nrm(ks[19], (N_ODD, CONV_W, W_C), CONV_W ** -0.5)
    lru_conv_b = nrm(ks[20], (N_ODD, W_C), 0.01)
    lru_w_a = nrm(ks[21], (N_ODD, H_C, BW_C, BW_C), BW_C ** -0.5)
    lru_b_a = nrm(ks[22], (N_ODD, W_C), 0.01)
    lru_w_x = nrm(ks[23], (N_ODD, H_C, BW_C, BW_C), BW_C ** -0.5)
    lru_b_x = nrm(ks[24], (N_ODD, W_C), 0.01)
    a8 = jax.random.uniform(ks[25], (N_ODD, W_C), f32, 0.9, 0.999)
    p = a8 ** (1.0 / RG_C)
    lru_lambda = jnp.log(p) - jnp.log1p(-p)
    w_out_odd = nrm(ks[26], (N_ODD, W_C, D_MODEL), W_C ** -0.5)
    return {"x_prompt": x_prompt, "x_sample": x_sample, "cache_sb_k": cache_sb_k, "cache_sb_v": cache_sb_v,
            "state_gdn": state_gdn, "state_gdn_conv": state_gdn_conv, "state_lru": state_lru,
            "state_lru_conv": state_lru_conv, "page_table": page_table, "meta_tokens": meta_tokens,
            "norm_w": norm_w, "final_norm_w": final_norm_w, "w_in_even": w_in_even, "gdn_conv_w": gdn_conv_w,
            "gdn_a_log": gdn_a_log, "gdn_dt_bias": gdn_dt_bias, "gdn_norm_w": gdn_norm_w, "sb_bias": sb_bias,
            "w_out_even": w_out_even, "w_in_odd": w_in_odd, "lru_conv_w": lru_conv_w, "lru_conv_b": lru_conv_b,
            "lru_w_a": lru_w_a, "lru_b_a": lru_b_a, "lru_w_x": lru_w_x, "lru_b_x": lru_b_x,
            "lru_lambda": lru_lambda, "w_out_odd": w_out_odd}


def reference(x_prompt, x_sample, cache_sb_k, cache_sb_v, state_gdn, state_gdn_conv, state_lru,
              state_lru_conv, page_table, meta_tokens, norm_w, final_norm_w, w_in_even, gdn_conv_w,
              gdn_a_log, gdn_dt_bias, gdn_norm_w, sb_bias, w_out_even, w_in_odd, lru_conv_w, lru_conv_b,
              lru_w_a, lru_b_a, lru_w_x, lru_b_x, lru_lambda, w_out_odd):
    bsz = x_prompt.shape[0]
    meta = jnp.broadcast_to(meta_tokens.astype(x_prompt.dtype)[None], (bsz, N_META, D_MODEL))
    xp = jnp.concatenate([meta, x_prompt], axis=1)
    xs = x_sample
    pk, pv, pS, pcg, ph, pcl = [], [], [], [], [], []
    sk, sv, sS, scg, sh, scl = [], [], [], [], [], []
    for layer in range(DEPTH):
        j = layer // 2
        if layer % 2 == 0:
            ew = (norm_w[layer], w_in_even[j], gdn_conv_w[j], gdn_a_log[j], gdn_dt_bias[j], gdn_norm_w[j],
                  sb_bias[j], w_out_even[j])
            xp, cb, S, kb, vb = even_layer(xp, *ew, jnp.zeros((bsz, CONV_W - 1, GDN_QKV), xp.dtype),
                                           gdn_prompt, sb_prompt)
            pk.append(kb); pv.append(vb); pS.append(S); pcg.append(cb)
            sb_fn = functools.partial(sb_sample, cache_k=cache_sb_k, cache_v=cache_sb_v,
                                      page_table=page_table, layer=j)
            xs, cb, S, kb, vb = even_layer(xs, *ew, state_gdn_conv[j],
                                           functools.partial(gdn_step, state_gdn[j]), sb_fn)
            sk.append(kb); sv.append(vb); sS.append(S); scg.append(cb)
        else:
            ow = (norm_w[layer], w_in_odd[j], lru_conv_w[j], lru_conv_b[j], lru_w_a[j], lru_b_a[j],
                  lru_w_x[j], lru_b_x[j], lru_lambda[j], w_out_odd[j])
            xp, cb, hl = odd_layer(xp, *ow, jnp.zeros((bsz, CONV_W - 1, W_C), xp.dtype),
                                   jnp.zeros((bsz, W_C), jnp.float32))
            pcl.append(cb); ph.append(hl)
            xs, cb, hl = odd_layer(xs, *ow, state_lru_conv[j], state_lru[j])
            scl.append(cb); sh.append(hl)
    y_prompt = rmsnorm(xp, final_norm_w)[:, N_META:]
    y_sample = rmsnorm(xs, final_norm_w)
    p_sb_k, p_sb_v, p_gdn, p_gdn_conv = jnp.stack(pk), jnp.stack(pv), jnp.stack(pS), jnp.stack(pcg)
    p_lru, p_lru_conv = jnp.stack(ph), jnp.stack(pcl)
    s_sb_k, s_sb_v, s_gdn, s_gdn_conv = jnp.stack(sk), jnp.stack(sv), jnp.stack(sS), jnp.stack(scg)
    s_lru, s_lru_conv = jnp.stack(sh), jnp.stack(scl)
    return (y_prompt, y_sample, p_sb_k, p_sb_v, p_gdn, p_gdn_conv, p_lru, p_lru_conv,
            s_sb_k, s_sb_v, s_gdn, s_gdn_conv, s_lru, s_lru_conv)
```

```python
import functools
import math

import jax
import jax.numpy as jnp
from jax import lax
from jax.experimental import pallas as pl
from jax.experimental.pallas import tpu as pltpu

EPS = 1e-6
RG_C = 8.0
GDN_CHUNK = 64
SB_BLOCK = 128
LRU_CHUNK = 256
SB_PAGES_PER_STEP = 4
F32 = jnp.float32
BF16 = jnp.bfloat16
V7X_VMEM_LIMIT_BYTES = 56 * 1024 * 1024
NEG_BIG = -1e30


def _cparams(*dims):
    return pltpu.CompilerParams(dimension_semantics=dims, vmem_limit_bytes=V7X_VMEM_LIMIT_BYTES)


def _row_tile(n, cap, mult=16):
    best = None
    for t in range(mult, min(n, cap) + 1, mult):
        if n % t == 0:
            best = t
    return best if best is not None else n


def _sigmoid(x):
    return 1.0 / (1.0 + jnp.exp(-x))


def _silu(x):
    return x * _sigmoid(x)


def _softplus(x):
    return jnp.maximum(x, 0.0) + jnp.log1p(jnp.exp(-jnp.abs(x)))


def _dot(a, b):
    return jnp.dot(a, b, preferred_element_type=F32)


def _dot_nt(a, b):
    return lax.dot_general(a, b, (((1,), (1,)), ((), ())), preferred_element_type=F32)


def _dot_tn(a, b):
    return lax.dot_general(a, b, (((0,), (0,)), ((), ())), preferred_element_type=F32)


def _shift_down(x, k, fill):
    rolled = pltpu.roll(x, k, 0)
    rows = lax.broadcasted_iota(jnp.int32, x.shape, 0)
    return jnp.where(rows >= k, rolled, fill)


def _inproj_kernel(*refs, tm, has_small):
    if has_small:
        x_ref, nw_ref, w_ref, ws_ref, o_ref, os_ref, xn_ref = refs
    else:
        x_ref, nw_ref, w_ref, o_ref, xn_ref = refs
    rc = 16 if tm % 16 == 0 else tm

    @pl.when(pl.program_id(1) == 0)
    def _():
        def body(c, carry):
            r = pl.multiple_of(c * rc, rc)
            xf = x_ref[pl.ds(r, rc), :]
            ms = jnp.mean(xf * xf, axis=-1, keepdims=True)
            xn_ref[pl.ds(r, rc), :] = (xf * lax.rsqrt(ms + EPS) * nw_ref[...]).astype(BF16)
            return carry

        lax.fori_loop(0, tm // rc, body, 0)
        if has_small:
            os_ref[...] = _dot(xn_ref[...], ws_ref[...])

    o_ref[...] = _dot(xn_ref[...], w_ref[...])


def _inproj(x, nw, w_bf, n_main, w_small_bf=None, tn=512, tm_cap=688):
    T, D = x.shape
    tm = _row_tile(T, tm_cap)
    has_small = w_small_bf is not None
    in_specs = [pl.BlockSpec((tm, D), lambda i, j: (i, 0)),
                pl.BlockSpec((1, D), lambda i, j: (0, 0)),
                pl.BlockSpec((D, tn), lambda i, j: (0, j))]
    out_shape = [jax.ShapeDtypeStruct((T, n_main), F32)]
    out_specs = [pl.BlockSpec((tm, tn), lambda i, j: (i, j))]
    args = [x, nw.reshape(1, D), w_bf]
    if has_small:
        ns = w_small_bf.shape[1]
        in_specs.append(pl.BlockSpec((D, ns), lambda i, j: (0, 0)))
        out_shape.append(jax.ShapeDtypeStruct((T, ns), F32))
        out_specs.append(pl.BlockSpec((tm, ns), lambda i, j: (i, 0)))
        args.append(w_small_bf)
    out = pl.pallas_call(
        functools.partial(_inproj_kernel, tm=tm, has_small=has_small),
        grid=(T // tm, n_main // tn),
        in_specs=in_specs, out_specs=out_specs, out_shape=out_shape,
        scratch_shapes=[pltpu.VMEM((tm, D), BF16)],
        compiler_params=_cparams("parallel", "arbitrary"),
        name="inproj",
    )(*args)
    return out if has_small else out[0]


def _outproj_kernel(*refs, n_parts):
    lhs_refs = refs[:n_parts]
    w_ref, x_ref, o_ref = refs[n_parts:]
    acc = x_ref[...]
    k0 = 0
    for lr in lhs_refs:
        kk = lr.shape[1]
        acc = acc + _dot(lr[...], w_ref[k0:k0 + kk, :])
        k0 += kk
    o_ref[...] = acc


def _outproj(lhs_parts, w_bf, x, tm_cap=688):
    T, D = x.shape
    tm = _row_tile(T, tm_cap)
    in_specs = [pl.BlockSpec((tm, p.shape[1]), lambda i: (i, 0)) for p in lhs_parts]
    in_specs += [pl.BlockSpec(w_bf.shape, lambda i: (0, 0)),
                 pl.BlockSpec((tm, D), lambda i: (i, 0))]
    return pl.pallas_call(
        functools.partial(_outproj_kernel, n_parts=len(lhs_parts)),
        grid=(T // tm,),
        in_specs=in_specs,
        out_specs=pl.BlockSpec((tm, D), lambda i: (i, 0)),
        out_shape=jax.ShapeDtypeStruct((T, D), F32),
        compiler_params=_cparams("parallel"),
        name="outproj",
    )(*lhs_parts, w_bf, x)


def _final_norm_kernel(x_ref, w_ref, o_ref, *, n_skip, tr):
    t = pl.program_id(1)
    r = pl.multiple_of(n_skip + t * tr, 8)
    xf = x_ref[0, pl.ds(r, tr), :]
    ms = jnp.mean(xf * xf, axis=-1, keepdims=True)
    o_ref[0] = xf * lax.rsqrt(ms + EPS) * w_ref[...]


def _final_norm(x, w, n_skip):
    B, L, D = x.shape
    Lo = L - n_skip
    tr = _row_tile(Lo, 512, 8)
    return pl.pallas_call(
        functools.partial(_final_norm_kernel, n_skip=n_skip, tr=tr),
        grid=(B, Lo // tr),
        in_specs=[pl.BlockSpec((1, L, D), lambda b, t: (b, 0, 0)),
                  pl.BlockSpec((1, D), lambda b, t: (0, 0))],
        out_specs=pl.BlockSpec((1, tr, D), lambda b, t: (b, t, 0)),
        out_shape=jax.ShapeDtypeStruct((B, Lo, D), F32),
        compiler_params=_cparams("parallel", "arbitrary"),
        name="final_norm",
    )(x, w.reshape(1, D))


def _conv_rows(win, cw, n):
    kw = cw.shape[0]
    out = win[8:8 + n] * cw[kw - 1:kw, :]
    for tau in range(1, kw):
        out = out + pltpu.roll(win, tau, 0)[8:8 + n] * cw[kw - 1 - tau:kw - tau, :]
    return out


def _gdn_prompt_kernel(alog_ref, dtb_ref, q_ref, k_ref, v_ref, cwq_ref, cwk_ref, cwv_ref,
                       braw_ref, araw_ref, gnw_ref, z_ref, o_ref, s_ref, S_scr,
                       *, n_meta, n_chunks, dk):
    h = pl.program_id(1)
    C = GDN_CHUNK
    a_coef = -jnp.exp(jnp.full((1, 1), alog_ref[h], F32))
    dt_b = dtb_ref[h]
    ii = lax.broadcasted_iota(jnp.int32, (C, C), 0)
    jj = lax.broadcasted_iota(jnp.int32, (C, C), 1)
    eye = (ii == jj).astype(F32)
    cwq, cwk, cwv = cwq_ref[...], cwk_ref[...], cwv_ref[...]
    gnw = gnw_ref[...]

    def load_win(ref, r0, first):
        if first:
            return jnp.concatenate([jnp.zeros((8, ref.shape[2]), F32), ref[0, 0:C, :]], axis=0)
        return ref[0, pl.ds(pl.multiple_of(r0 - 8, 8), C + 8), :]

    def chunk(S, r0, crow, first):
        q = _silu(_conv_rows(load_win(q_ref, r0, first), cwq, C))
        k = _silu(_conv_rows(load_win(k_ref, r0, first), cwk, C))
        v = _silu(_conv_rows(load_win(v_ref, r0, first), cwv, C))
        q = q * (lax.rsqrt(jnp.sum(q * q, axis=-1, keepdims=True) + EPS) * (dk ** -0.5))
        k = k * lax.rsqrt(jnp.sum(k * k, axis=-1, keepdims=True) + EPS)
        braw = braw_ref[0, 0, pl.ds(crow, 1), :]
        araw = araw_ref[0, 0, pl.ds(crow, 1), :]
        beta_row = _sigmoid(braw)
        g_row = a_coef * _softplus(araw + dt_b)
        if first:
            valid = lax.broadcasted_iota(jnp.int32, (1, C), 1) < n_meta
            beta_row = jnp.where(valid, beta_row, 0.0)
            g_row = jnp.where(valid, g_row, 0.0)
        g_rb = jnp.broadcast_to(g_row, (C, C))
        gc_col = jnp.sum(jnp.where(jj <= ii, g_rb, 0.0), axis=1, keepdims=True)
        g_col = jnp.sum(jnp.where(jj == ii, g_rb, 0.0), axis=1, keepdims=True)
        gc_row = jnp.sum(jnp.where(ii <= jj, jnp.broadcast_to(g_col, (C, C)), 0.0),
                         axis=0, keepdims=True)
        beta_col = jnp.sum(jnp.where(jj == ii, jnp.broadcast_to(beta_row, (C, C)), 0.0),
                           axis=1, keepdims=True)
        g_last = jnp.sum(g_row, axis=1, keepdims=True)
        decay = jnp.exp(jnp.where(ii >= jj, gc_col - gc_row, NEG_BIG))
        e_gc = jnp.exp(gc_col)
        kb = k * beta_col
        k_bf = k.astype(BF16)
        X = _dot_nt(jnp.concatenate([q, kb], axis=0).astype(BF16), k_bf)
        qk = X[:C] * decay
        N = -jnp.where(ii > jj, X[C:] * decay, 0.0)
        Tinv = eye + N
        Npow = N
        steps = max(1, int(math.ceil(math.log2(C))) - 1)
        for _ in range(steps):
            Npow_bf = Npow.astype(BF16)
            Npow = _dot(Npow_bf, Npow_bf)
            Tinv = Tinv + _dot(Tinv.astype(BF16), Npow.astype(BF16))
        rhs = jnp.concatenate([v * beta_col, kb * e_gc], axis=1)
        sol = _dot(Tinv.astype(BF16), rhs.astype(BF16))
        dv = v.shape[1]
        u, w = sol[:, :dv], sol[:, dv:]
        WQ = _dot(jnp.concatenate([w, q * e_gc], axis=0).astype(BF16), S.astype(BF16))
        v_new = u - WQ[:C]
        o = WQ[C:] + _dot(qk.astype(BF16), v_new.astype(BF16))
        kt = k * jnp.exp(g_last - gc_col)
        S_new = S * jnp.exp(g_last) + _dot_tn(kt.astype(BF16), v_new.astype(BF16))
        on = o * lax.rsqrt(jnp.mean(o * o, axis=-1, keepdims=True) + EPS) * gnw
        return S_new, on

    S0 = jnp.zeros(S_scr.shape, F32)
    S1, on = chunk(S0, 0, 0, True)
    o_ref[0, 0:n_meta, :] = (on[:n_meta] * _silu(z_ref[0, 0:n_meta, :])).astype(o_ref.dtype)
    S_scr[...] = S1

    def body(c, carry):
        r0 = pl.multiple_of(n_meta + c * C, 16)
        S_new, on = chunk(S_scr[...], r0, c + 1, False)
        o_ref[0, pl.ds(r0, C), :] = (on * _silu(z_ref[0, pl.ds(r0, C), :])).astype(o_ref.dtype)
        S_scr[...] = S_new
        return carry

    lax.fori_loop(0, n_chunks, body, 0)
    s_ref[0, 0] = S_scr[...]


def _gdn_prompt(P, gates_c, conv_w, a_log, dt_bias, gn_w, n_meta, H, dk, dv, off_z):
    B, L, _ = P.shape
    n_chunks = (L - n_meta) // GDN_CHUNK
    zb0 = off_z // dv
    cw = conv_w.shape[0]
    spec_col = lambda off: pl.BlockSpec((1, L, dk), lambda b, h: (b, 0, off + h))
    spec_cw = lambda off: pl.BlockSpec((cw, dk), lambda b, h: (0, off + h))
    smem = pl.BlockSpec(memory_space=pltpu.SMEM)
    nc1 = gates_c.shape[2]
    return pl.pallas_call(
        functools.partial(_gdn_prompt_kernel, n_meta=n_meta, n_chunks=n_chunks, dk=dk),
        grid=(B, H),
        in_specs=[smem, smem, spec_col(0), spec_col(H), spec_col(2 * H),
                  spec_cw(0), spec_cw(H), spec_cw(2 * H),
                  pl.BlockSpec((1, 1, nc1, GDN_CHUNK), lambda b, h: (b, h, 0, 0)),
                  pl.BlockSpec((1, 1, nc1, GDN_CHUNK), lambda b, h: (b, H + h, 0, 0)),
                  pl.BlockSpec((1, dv), lambda b, h: (0, 0)),
                  pl.BlockSpec((1, L, dv), lambda b, h: (b, 0, zb0 + h))],
        out_specs=[pl.BlockSpec((1, L, dv), lambda b, h: (b, 0, h)),
                   pl.BlockSpec((1, 1, dk, dv), lambda b, h: (b, h, 0, 0))],
        out_shape=[jax.ShapeDtypeStruct((B, L, H * dv), BF16),
                   jax.ShapeDtypeStruct((B, H, dk, dv), F32)],
        scratch_shapes=[pltpu.VMEM((dk, dv), F32)],
        compiler_params=_cparams("parallel", "arbitrary"),
        name="gdn_prompt",
    )(a_log, dt_bias, P, P, P, conv_w, conv_w, conv_w, gates_c, gates_c,
      gn_w.reshape(1, dv), P)


def _sb_prompt_kernel(bias_ref, q_ref, k_ref, v_ref, z_ref, o_ref, ko_ref, vo_ref,
                      *, n_meta, n_qb, dh):
    h = pl.program_id(1)
    BL = SB_BLOCK
    bias = bias_ref[h]
    scale = dh ** -0.5
    ii = lax.broadcasted_iota(jnp.int32, (BL, BL), 0)
    jj = lax.broadcasted_iota(jnp.int32, (BL, BL), 1)
    U = (ii > jj).astype(BF16)
    vis_diag = jj < ii
    vis_meta = jj < n_meta

    ko_ref[...] = k_ref[...]
    vo_ref[...] = v_ref[...]

    def pair(q_bf, ks, vis, carry):
        acc, c = carry
        kblk = k_ref[0, pl.ds(ks, BL), :].astype(BF16)
        vblk = v_ref[0, pl.ds(ks, BL), :].astype(BF16)
        z = _dot_nt(q_bf, kblk) + bias
        sp = jnp.maximum(z, 0.0) + jnp.log(1.0 + jnp.exp(-jnp.abs(z)))
        l1m = -sp if vis is None else jnp.where(vis, -sp, 0.0)
        hi = l1m.astype(BF16)
        lo = (l1m - hi.astype(F32)).astype(BF16)
        E2 = _dot(jnp.concatenate([hi, lo], axis=0), U)
        E = E2[:BL] + E2[BL:]
        tot = E[:, 0:1] + l1m[:, 0:1]
        a = jnp.exp(z - sp + E + c)
        if vis is not None:
            a = jnp.where(vis, a, 0.0)
        acc = acc + _dot(a.astype(BF16), vblk)
        return acc, c + tot

    zero = (jnp.zeros((BL, dh), F32), jnp.zeros((BL, 1), F32))

    q0 = (q_ref[0, 0:BL, :] * scale).astype(BF16)
    acc0, _ = pair(q0, 0, vis_diag, zero)
    o_ref[0, 0:n_meta, :] = (acc0[:n_meta] * _silu(z_ref[0, 0:n_meta, :])).astype(o_ref.dtype)

    def qblock(i, carry):
        qs = pl.multiple_of(n_meta + i * BL, 16)
        q_bf = (q_ref[0, pl.ds(qs, BL), :] * scale).astype(BF16)
        st = pair(q_bf, qs, vis_diag, zero)

        def kstep(t, st):
            ks = pl.multiple_of(n_meta + (i - 1 - t) * BL, 16)
            return pair(q_bf, ks, None, st)

        st = lax.fori_loop(0, i, kstep, st)
        acc, _ = pair(q_bf, 0, vis_meta, st)
        o_ref[0, pl.ds(qs, BL), :] = (acc * _silu(z_ref[0, pl.ds(qs, BL), :])).astype(o_ref.dtype)
        return carry

    lax.fori_loop(0, n_qb, qblock, 0)


def _sb_prompt(P, sb_bias, n_meta, H, dh, off_q, off_z):
    B, L, _ = P.shape
    n_qb = (L - n_meta) // SB_BLOCK
    qb0, zb0 = off_q // dh, off_z // dh
    col = lambda off: pl.BlockSpec((1, L, dh), lambda b, h: (b, 0, off + h))
    return pl.pallas_call(
        functools.partial(_sb_prompt_kernel, n_meta=n_meta, n_qb=n_qb, dh=dh),
        grid=(B, H),
        in_specs=[pl.BlockSpec(memory_space=pltpu.SMEM),
                  col(qb0), col(qb0 + H), col(qb0 + 2 * H), col(zb0)],
        out_specs=[col(0), col(0), col(0)],
        out_shape=[jax.ShapeDtypeStruct((B, L, H * dh), BF16),
                   jax.ShapeDtypeStruct((B, L, H * dh), F32),
                   jax.ShapeDtypeStruct((B, L, H * dh), F32)],
        compiler_params=_cparams("parallel", "arbitrary"),
        name="sb_prompt",
    )(sb_bias, P, P, P, P)


def _lru_gates(xc, w_ref, ba, bx, ls_lam):
    bw = xc.shape[1]
    pre = _dot(xc.astype(BF16), w_ref[0])
    r = _sigmoid(pre[:, :bw] + ba)
    i = _sigmoid(pre[:, bw:] + bx)
    log_a = RG_C * r * ls_lam
    a = jnp.exp(log_a)
    b = jnp.sqrt(1.0 - a * a) * (i * xc)
    return a, b


def _lru_prompt_kernel(x_ref, g_ref, cw_ref, cb_ref, w_ref, ba_ref, bx_ref, lam_ref,
                       y_ref, hl_ref, h_scr, *, n_meta, n_chunks):
    TC = LRU_CHUNK
    cw = cw_ref[...]
    cb = cb_ref[...]
    ba, bx = ba_ref[...], bx_ref[...]
    ls_lam = -_softplus(-lam_ref[...])
    bw = x_ref.shape[2]

    def scan_chunk(win, n, r0, h_prev):
        xc = _conv_rows(win, cw, n) + cb
        a, b = _lru_gates(xc, w_ref, ba, bx, ls_lam)
        k = 1
        while k < n:
            a_s = _shift_down(a, k, 1.0)
            b_s = _shift_down(b, k, 0.0)
            b = a * b_s + b
            a = a * a_s
            k *= 2
        hrows = a * h_prev + b
        y_ref[0, pl.ds(r0, n), :] = (hrows * _silu(g_ref[0, pl.ds(r0, n), :])).astype(y_ref.dtype)
        return hrows[n - 1:n, :]

    win0 = jnp.concatenate([jnp.zeros((8, bw), F32), x_ref[0, 0:n_meta, :]], axis=0)
    h_scr[...] = scan_chunk(win0, n_meta, 0, jnp.zeros((1, bw), F32))

    def body(c, carry):
        r0 = pl.multiple_of(n_meta + c * TC, 16)
        win = x_ref[0, pl.ds(pl.multiple_of(r0 - 8, 8), TC + 8), :]
        h_scr[...] = scan_chunk(win, TC, r0, h_scr[...])
        return carry

    lax.fori_loop(0, n_chunks, body, 0)
    hl_ref[0] = h_scr[...]


def _lru_prompt(P, conv_w, conv_b, w_ax_bf, b_a, b_x, lam, n_meta, w_c):
    B, L, _ = P.shape
    hc, bw, _ = w_ax_bf.shape
    n_chunks = (L - n_meta) // LRU_CHUNK
    kw = conv_w.shape[0]
    vec = lambda a: a.reshape(1, w_c)
    vspec = pl.BlockSpec((1, bw), lambda b, h: (0, h))
    return pl.pallas_call(
        functools.partial(_lru_prompt_kernel, n_meta=n_meta, n_chunks=n_chunks),
        grid=(B, hc),
        in_specs=[pl.BlockSpec((1, L, bw), lambda b, h: (b, 0, h)),
                  pl.BlockSpec((1, L, bw), lambda b, h: (b, 0, hc + h)),
                  pl.BlockSpec((kw, bw), lambda b, h: (0, h)),
                  vspec,
                  pl.BlockSpec((1, bw, 2 * bw), lambda b, h: (h, 0, 0)),
                  vspec, vspec, vspec],
        out_specs=[pl.BlockSpec((1, L, bw), lambda b, h: (b, 0, h)),
                   pl.BlockSpec((1, 1, bw), lambda b, h: (b, 0, h))],
        out_shape=[jax.ShapeDtypeStruct((B, L, w_c), BF16),
                   jax.ShapeDtypeStruct((B, 1, w_c), F32)],
        scratch_shapes=[pltpu.VMEM((1, bw), F32)],
        compiler_params=_cparams("parallel", "arbitrary"),
        name="lru_prompt",
    )(P, P, conv_w, vec(conv_b), w_ax_bf, vec(b_a), vec(b_x), vec(lam))


def _gdn_sample_kernel(alog_ref, dtb_ref, u_ref, buf_ref, cw_ref, gates_ref, gnw_ref, z_ref, s_ref,
                       o_ref, so_ref, *, H, dk, dv):
    kw = cw_ref.shape[0]
    xc = u_ref[0] * cw_ref[kw - 1:kw, :]
    for t in range(kw - 1):
        xc = xc + buf_ref[0, t:t + 1, :] * cw_ref[t:t + 1, :]
    xc = _silu(xc)
    gates = gates_ref[0]
    gnw = gnw_ref[...]
    row8 = lax.broadcasted_iota(jnp.int32, (8, dk), 0)
    di = lax.broadcasted_iota(jnp.int32, (dk, dk), 0)
    dj = lax.broadcasted_iota(jnp.int32, (dk, dk), 1)
    for h in range(H):
        q = xc[:, h * dk:(h + 1) * dk]
        k = xc[:, (H + h) * dk:(H + h + 1) * dk]
        v = xc[:, 2 * H * dk + h * dv:2 * H * dk + (h + 1) * dv]
        q = q * (lax.rsqrt(jnp.sum(q * q, axis=-1, keepdims=True) + EPS) * (dk ** -0.5))
        k = k * lax.rsqrt(jnp.sum(k * k, axis=-1, keepdims=True) + EPS)
        beta = _sigmoid(gates[:, h:h + 1])
        g = -jnp.exp(jnp.full((1, 1), alog_ref[h], F32)) * _softplus(gates[:, H + h:H + h + 1] + dtb_ref[h])
        eg = jnp.exp(g)
        S = s_ref[0, h]
        kq = jnp.where(row8 == 0, jnp.broadcast_to(k, (8, dk)),
                       jnp.where(row8 == 1, jnp.broadcast_to(q * eg, (8, dk)), 0.0))
        R = _dot(kq.astype(BF16), S.astype(BF16))
        v_new = beta * (v - eg * R[0:1])
        o = R[1:2] + jnp.sum(q * k, axis=-1, keepdims=True) * v_new
        k_col = jnp.sum(jnp.where(di == dj, jnp.broadcast_to(k, (dk, dk)), 0.0),
                        axis=1, keepdims=True)
        so_ref[0, h] = S * eg + k_col * v_new
        on = o * lax.rsqrt(jnp.mean(o * o, axis=-1, keepdims=True) + EPS) * gnw
        o_ref[0, :, h * dv:(h + 1) * dv] = (on * _silu(z_ref[0, :, h * dv:(h + 1) * dv])).astype(o_ref.dtype)


def _gdn_sample(u, buf, conv_w, gates, a_log, dt_bias, gn_w, z_a, S, H, dk, dv):
    Bd, C = u.shape
    kw = conv_w.shape[0]
    smem = pl.BlockSpec(memory_space=pltpu.SMEM)
    row = lambda n: pl.BlockSpec((1, 1, n), lambda b: (b, 0, 0))
    o, s_new = pl.pallas_call(
        functools.partial(_gdn_sample_kernel, H=H, dk=dk, dv=dv),
        grid=(Bd,),
        in_specs=[smem, smem, row(C),
                  pl.BlockSpec((1, kw - 1, C), lambda b: (b, 0, 0)),
                  pl.BlockSpec((kw, C), lambda b: (0, 0)),
                  row(2 * H),
                  pl.BlockSpec((1, dv), lambda b: (0, 0)),
                  row(H * dv),
                  pl.BlockSpec((1, H, dk, dv), lambda b: (b, 0, 0, 0))],
        out_specs=[row(H * dv), pl.BlockSpec((1, H, dk, dv), lambda b: (b, 0, 0, 0))],
        out_shape=[jax.ShapeDtypeStruct((Bd, 1, H * dv), BF16),
                   jax.ShapeDtypeStruct((Bd, H, dk, dv), F32)],
        compiler_params=_cparams("parallel"),
        name="gdn_sample",
    )(a_log, dt_bias, u.reshape(Bd, 1, C), buf, conv_w, gates.reshape(Bd, 1, 2 * H),
      gn_w.reshape(1, dv), z_a.reshape(Bd, 1, H * dv), S)
    return o.reshape(Bd, H * dv), s_new


def _sb_sample_kernel(pt_ref, q_ref, bias_ref, z_ref, *refs, H, dh, page, pps, n_steps):
    k_refs = refs[:pps]
    v_refs = refs[pps:2 * pps]
    o_ref, acc_scr, c_scr = refs[2 * pps:]
    p = pl.program_id(1)
    LW = 128
    nv = (page * H) // LW

    @pl.when(p == 0)
    def _():
        acc_scr[...] = jnp.zeros(acc_scr.shape, F32)
        c_scr[...] = jnp.zeros(c_scr.shape, F32)

    q_bf = (q_ref[0] * (dh ** -0.5)).astype(BF16)
    bias = bias_ref[...]
    lane = lax.broadcasted_iota(jnp.int32, (H, LW), 1)
    sub = lax.broadcasted_iota(jnp.int32, (H, LW), 0)
    valid = (lane % H) == sub
    ui = lax.broadcasted_iota(jnp.int32, (LW, LW), 0)
    uj = lax.broadcasted_iota(jnp.int32, (LW, LW), 1)
    U = (ui > uj).astype(BF16)

    acc = acc_scr[...]
    c = c_scr[...]
    for s in range(pps):
        kp = k_refs[s][0, 0].astype(BF16)
        vp = v_refs[s][0, 0].astype(BF16)
        z = _dot_nt(q_bf, kp) + bias
        sp = jnp.maximum(z, 0.0) + jnp.log(1.0 + jnp.exp(-jnp.abs(z)))
        ls = z - sp
        l1m = [jnp.where(valid, -sp[:, i * LW:(i + 1) * LW], 0.0) for i in range(nv)]
        L = jnp.concatenate(l1m, axis=0)
        hi = L.astype(BF16)
        lo = (L - hi.astype(F32)).astype(BF16)
        E2 = _dot(jnp.concatenate([hi, lo], axis=0), U)
        E = E2[:nv * H] + E2[nv * H:]
        tot = E[:, 0:1] + L[:, 0:1]
        a_parts = [None] * nv
        for i in range(nv - 1, -1, -1):
            loga = ls[:, i * LW:(i + 1) * LW] + E[i * H:(i + 1) * H] + c
            a_parts[i] = jnp.where(valid, jnp.exp(loga), 0.0)
            c = c + tot[i * H:(i + 1) * H]
        a = jnp.concatenate(a_parts, axis=1).astype(BF16)
        acc = acc + _dot(a, vp)
    acc_scr[...] = acc
    c_scr[...] = c

    @pl.when(p == n_steps - 1)
    def _():
        o_ref[0] = (acc * _silu(z_ref[0])).astype(o_ref.dtype)


def _sb_sample(q, z_b, sb_bias, cache_k, cache_v, page_table, layer):
    Bd, H, dh = q.shape
    _, n_phys, page, _, _ = cache_k.shape
    n_pages = page_table.shape[1]
    pps = SB_PAGES_PER_STEP if n_pages % SB_PAGES_PER_STEP == 0 else 1
    n_steps = n_pages // pps
    ck = cache_k.reshape(cache_k.shape[0], n_phys, page * H, dh)
    cv = cache_v.reshape(cache_v.shape[0], n_phys, page * H, dh)

    def page_spec(s):
        def imap(b, p, pt):
            return (layer, pt[b * n_pages + (n_pages - 1 - (p * pps + s))], 0, 0)
        return pl.BlockSpec((1, 1, page * H, dh), imap)

    hd = pl.BlockSpec((1, H, dh), lambda b, p, pt: (b, 0, 0))
    grid_spec = pltpu.PrefetchScalarGridSpec(
        num_scalar_prefetch=1,
        grid=(Bd, n_steps),
        in_specs=[hd, pl.BlockSpec((H, 1), lambda b, p, pt: (0, 0)), hd]
                 + [page_spec(s) for s in range(pps)] * 2,
        out_specs=hd,
        scratch_shapes=[pltpu.VMEM((H, dh), F32), pltpu.VMEM((H, 1), F32)],
    )
    return pl.pallas_call(
        functools.partial(_sb_sample_kernel, H=H, dh=dh, page=page, pps=pps, n_steps=n_steps),
        grid_spec=grid_spec,
        out_shape=jax.ShapeDtypeStruct((Bd, H, dh), BF16),
        compiler_params=_cparams("parallel", "arbitrary"),
        name="sb_sample",
    )(page_table.reshape(-1), q, sb_bias.reshape(H, 1), z_b, *([ck] * pps), *([cv] * pps))


def _lru_sample_kernel(x_ref, g_ref, buf_ref, cw_ref, cb_ref, w_ref, ba_ref, bx_ref, lam_ref, h0_ref,
                       y_ref, h_ref):
    kw = cw_ref.shape[0]
    xc = x_ref[...] * cw_ref[kw - 1:kw, :] + cb_ref[...]
    for t in range(kw - 1):
        xc = xc + buf_ref[t] * cw_ref[t:t + 1, :]
    ls_lam = -_softplus(-lam_ref[...])
    a, b = _lru_gates(xc, w_ref, ba_ref[...], bx_ref[...], ls_lam)
    hn = a * h0_ref[...] + b
    h_ref[...] = hn
    y_ref[...] = (hn * _silu(g_ref[...])).astype(y_ref.dtype)


def _lru_sample(P, buf_t, conv_w, conv_b, w_ax_bf, b_a, b_x, lam, h0, w_c):
    Bd = P.shape[0]
    hc, bw, _ = w_ax_bf.shape
    kw = conv_w.shape[0]
    vec = lambda a: a.reshape(1, w_c)
    vspec = pl.BlockSpec((1, bw), lambda h: (0, h))
    mspec = lambda off: pl.BlockSpec((Bd, bw), lambda h: (0, off + h))
    return pl.pallas_call(
        _lru_sample_kernel,
        grid=(hc,),
        in_specs=[mspec(0), mspec(hc),
                  pl.BlockSpec((kw - 1, Bd, bw), lambda h: (0, 0, h)),
                  pl.BlockSpec((kw, bw), lambda h: (0, h)),
                  vspec,
                  pl.BlockSpec((1, bw, 2 * bw), lambda h: (h, 0, 0)),
                  vspec, vspec, vspec, mspec(0)],
        out_specs=[mspec(0), mspec(0)],
        out_shape=[jax.ShapeDtypeStruct((Bd, w_c), BF16), jax.ShapeDtypeStruct((Bd, w_c), F32)],
        compiler_params=_cparams("parallel"),
        name="lru_sample",
    )(P, P, buf_t, conv_w, vec(conv_b), w_ax_bf, vec(b_a), vec(b_x), vec(lam), h0)


def kernel(x_prompt, x_sample, cache_sb_k, cache_sb_v, state_gdn, state_gdn_conv, state_lru,
           state_lru_conv, page_table, meta_tokens, norm_w, final_norm_w, w_in_even, gdn_conv_w,
           gdn_a_log, gdn_dt_bias, gdn_norm_w, sb_bias, w_out_even, w_in_odd, lru_conv_w, lru_conv_b,
           lru_w_a, lru_b_a, lru_w_x, lru_b_x, lru_lambda, w_out_odd):
    B, seq, D = x_prompt.shape
    Bd = x_sample.shape[0]
    assert x_sample.shape[1] == 1
    n_meta = meta_tokens.shape[0]
    L = n_meta + seq
    depth = norm_w.shape[0]
    _, _, H_a, dk, dv = state_gdn.shape
    _, _, _, H_b, dh = cache_sb_k.shape
    w_a_tot, w_b_tot = H_a * dv, H_b * dh
    qkv_a = 2 * H_a * dk + H_a * dv
    off_qkv_b = qkv_a
    off_z_a = off_qkv_b + 3 * w_b_tot
    off_z_b = off_z_a + w_a_tot
    off_gate = off_z_b + w_b_tot
    w_c = state_lru.shape[2]
    assert dk == dv == dh == 128 and n_meta % 16 == 0 and n_meta <= GDN_CHUNK
    assert seq % LRU_CHUNK == 0 and seq % SB_BLOCK == 0 and seq % GDN_CHUNK == 0
    assert w_in_even.shape[2] == off_gate + 2 * H_a and lru_w_a.shape[2] == 128

    meta = jnp.broadcast_to(meta_tokens.astype(x_prompt.dtype)[None], (B, n_meta, D))
    xp = jnp.concatenate([meta, x_prompt], axis=1).reshape(B * L, D)
    xs = x_sample.reshape(Bd, D)

    n_chunks = seq // GDN_CHUNK
    pk, pv, pS, pcg, ph, pcl = [], [], [], [], [], []
    sk, sv, sS, scg, sh, scl = [], [], [], [], [], []
    for layer in range(depth):
        j = layer // 2
        if layer % 2 == 0:
            w_bf = w_in_even[j].astype(BF16)
            w_gate_bf = w_bf[:, off_gate:]
            w_out_bf = w_out_even[j].astype(BF16)
            P, gates = _inproj(xp, norm_w[layer], w_bf, off_gate, w_gate_bf)
            P3 = P.reshape(B, L, off_gate)
            g3 = jnp.swapaxes(gates.reshape(B, L, 2 * H_a), 1, 2)
            g_meta = jnp.pad(g3[:, :, :n_meta], ((0, 0), (0, 0), (0, GDN_CHUNK - n_meta)))
            gates_c = jnp.concatenate(
                [g_meta[:, :, None, :], g3[:, :, n_meta:].reshape(B, 2 * H_a, n_chunks, GDN_CHUNK)], axis=2)
            la, S_p = _gdn_prompt(P3, gates_c, gdn_conv_w[j], gdn_a_log[j], gdn_dt_bias[j], gdn_norm_w[j],
                                  n_meta, H_a, dk, dv, off_z_a)
            lb, k_p, v_p = _sb_prompt(P3, sb_bias[j], n_meta, H_b, dh, off_qkv_b, off_z_b)
            xp = _outproj([la.reshape(B * L, w_a_tot), lb.reshape(B * L, w_b_tot)], w_out_bf, xp)
            pk.append(k_p.reshape(B, L, H_b, dh)); pv.append(v_p.reshape(B, L, H_b, dh))
            pS.append(S_p); pcg.append(P3[:, L - (gdn_conv_w.shape[1] - 1):, :qkv_a])
            Ps, gs = _inproj(xs, norm_w[layer], w_bf, off_gate, w_gate_bf)
            u = Ps[:, :qkv_a]
            las, S_s = _gdn_sample(u, state_gdn_conv[j], gdn_conv_w[j], gs, gdn_a_log[j], gdn_dt_bias[j],
                                   gdn_norm_w[j], Ps[:, off_z_a:off_z_a + w_a_tot], state_gdn[j], H_a, dk, dv)
            q_s = Ps[:, off_qkv_b:off_qkv_b + w_b_tot].reshape(Bd, H_b, dh)
            k_s = Ps[:, off_qkv_b + w_b_tot:off_qkv_b + 2 * w_b_tot]
            v_s = Ps[:, off_qkv_b + 2 * w_b_tot:off_qkv_b + 3 * w_b_tot]
            z_s = Ps[:, off_z_b:off_z_b + w_b_tot].reshape(Bd, H_b, dh)
            lbs = _sb_sample(q_s, z_s, sb_bias[j], cache_sb_k, cache_sb_v, page_table, j)
            xs = _outproj([las, lbs.reshape(Bd, w_b_tot)], w_out_bf, xs)
            sk.append(k_s.reshape(Bd, 1, H_b, dh)); sv.append(v_s.reshape(Bd, 1, H_b, dh))
            sS.append(S_s)
            scg.append(jnp.concatenate([state_gdn_conv[j][:, 1:], u[:, None, :]], axis=1))
        else:
            w_bf = w_in_odd[j].astype(BF16)
            w_out_bf = w_out_odd[j].astype(BF16)
            w_ax_bf = jnp.concatenate([lru_w_a[j], lru_w_x[j]], axis=-1).astype(BF16)
            lru_args = (lru_conv_w[j], lru_conv_b[j], w_ax_bf, lru_b_a[j], lru_b_x[j], lru_lambda[j])
            P = _inproj(xp, norm_w[layer], w_bf, 2 * w_c)
            P3 = P.reshape(B, L, 2 * w_c)
            y, h_last = _lru_prompt(P3, *lru_args, n_meta, w_c)
            xp = _outproj([y.reshape(B * L, w_c)], w_out_bf, xp)
            ph.append(h_last.reshape(B, w_c)); pcl.append(P3[:, L - (lru_conv_w.shape[1] - 1):, :w_c])
            Ps = _inproj(xs, norm_w[layer], w_bf, 2 * w_c)
            ys, h_s = _lru_sample(Ps, jnp.swapaxes(state_lru_conv[j], 0, 1), *lru_args, state_lru[j], w_c)
            xs = _outproj([ys], w_out_bf, xs)
            sh.append(h_s)
            scl.append(jnp.concatenate([state_lru_conv[j][:, 1:], Ps[:, None, :w_c]], axis=1))

    y_prompt = _final_norm(xp.reshape(B, L, D), final_norm_w, n_meta)
    y_sample = _final_norm(xs.reshape(1, Bd, D), final_norm_w, 0).reshape(Bd, 1, D)
    st = jnp.stack
    return (y_prompt, y_sample, st(pk), st(pv), st(pS), st(pcg), st(ph), st(pcl),
            st(sk), st(sv), st(sS), st(scg), st(sh), st(scl))
```

```python
import functools
import math

import jax
import jax.numpy as jnp
from jax import lax
from jax.experimental import pallas as pl
from jax.experimental.pallas import tpu as pltpu

EPS = 1e-6
RG_C = 8.0
GDN_CHUNK = 64
SB_BLOCK = 128
SB_GROUP = 256
LRU_CHUNK = 256
GDN_HEADS_PER_STEP = 4
GDN_CHUNKS_PER_ITER = 4
SB_PAGES_PER_STEP = 8
F32 = jnp.float32
BF16 = jnp.bfloat16
V7X_VMEM_LIMIT_BYTES = 56 * 1024 * 1024
NEG_BIG = -1e30


def _cparams(*dims):
    return pltpu.CompilerParams(dimension_semantics=dims, vmem_limit_bytes=V7X_VMEM_LIMIT_BYTES)


def _row_tile(n, cap, mult=16):
    best = None
    for t in range(mult, min(n, cap) + 1, mult):
        if n % t == 0:
            best = t
    return best if best is not None else n


def _sigmoid(x):
    return 1.0 / (1.0 + jnp.exp(-x))


def _silu(x):
    return x * _sigmoid(x)


def _softplus(x):
    return jnp.maximum(x, 0.0) + jnp.log1p(jnp.exp(-jnp.abs(x)))


def _dot(a, b):
    return jnp.dot(a, b, preferred_element_type=F32)


def _dot_nt(a, b):
    return lax.dot_general(a, b, (((1,), (1,)), ((), ())), preferred_element_type=F32)


def _dot_tn(a, b):
    return lax.dot_general(a, b, (((0,), (0,)), ((), ())), preferred_element_type=F32)


def _shift_down(x, k, fill):
    if k % 8 == 0:
        return jnp.concatenate([jnp.full((k, x.shape[1]), fill, x.dtype), x[:x.shape[0] - k]], axis=0)
    rolled = pltpu.roll(x, k, 0)
    rows = lax.broadcasted_iota(jnp.int32, x.shape, 0)
    return jnp.where(rows >= k, rolled, fill)


def _inproj_kernel(*refs, tm, has_small):
    if has_small:
        x_ref, nw_ref, w_ref, ws_ref, o_ref, os_ref, xn_ref = refs
    else:
        x_ref, nw_ref, w_ref, o_ref, xn_ref = refs
    rc = 16 if tm % 16 == 0 else tm

    @pl.when(pl.program_id(1) == 0)
    def _():
        def body(c, carry):
            r = pl.multiple_of(c * rc, rc)
            xf = x_ref[pl.ds(r, rc), :]
            ms = jnp.mean(xf * xf, axis=-1, keepdims=True)
            xn_ref[pl.ds(r, rc), :] = (xf * lax.rsqrt(ms + EPS) * nw_ref[...]).astype(BF16)
            return carry

        lax.fori_loop(0, tm // rc, body, 0)
        if has_small:
            os_ref[...] = _dot(xn_ref[...], ws_ref[...])

    o_ref[...] = _dot(xn_ref[...], w_ref[...])


def _inproj(x, nw, w_bf, n_main, w_small_bf=None, tn=512, tm_cap=2064):
    T, D = x.shape
    tm = _row_tile(T, tm_cap)
    has_small = w_small_bf is not None
    in_specs = [pl.BlockSpec((tm, D), lambda i, j: (i, 0)),
                pl.BlockSpec((1, D), lambda i, j: (0, 0)),
                pl.BlockSpec((D, tn), lambda i, j: (0, j))]
    out_shape = [jax.ShapeDtypeStruct((T, n_main), F32)]
    out_specs = [pl.BlockSpec((tm, tn), lambda i, j: (i, j))]
    args = [x, nw.reshape(1, D), w_bf]
    if has_small:
        ns = w_small_bf.shape[1]
        in_specs.append(pl.BlockSpec((D, ns), lambda i, j: (0, 0)))
        out_shape.append(jax.ShapeDtypeStruct((T, ns), F32))
        out_specs.append(pl.BlockSpec((tm, ns), lambda i, j: (i, 0)))
        args.append(w_small_bf)
    out = pl.pallas_call(
        functools.partial(_inproj_kernel, tm=tm, has_small=has_small),
        grid=(T // tm, n_main // tn),
        in_specs=in_specs, out_specs=out_specs, out_shape=out_shape,
        scratch_shapes=[pltpu.VMEM((tm, D), BF16)],
        compiler_params=_cparams("parallel", "arbitrary"),
        name="inproj",
    )(*args)
    return out if has_small else out[0]


def _outproj_kernel(*refs, n_parts):
    lhs_refs = refs[:n_parts]
    w_ref, x_ref, o_ref = refs[n_parts:]
    acc = x_ref[...]
    k0 = 0
    for lr in lhs_refs:
        kk = lr.shape[1]
        acc = acc + _dot(lr[...], w_ref[k0:k0 + kk, :])
        k0 += kk
    o_ref[...] = acc


def _outproj(lhs_parts, w_bf, x, tm_cap=688):
    T, D = x.shape
    tm = _row_tile(T, tm_cap)
    in_specs = [pl.BlockSpec((tm, p.shape[1]), lambda i: (i, 0)) for p in lhs_parts]
    in_specs += [pl.BlockSpec(w_bf.shape, lambda i: (0, 0)),
                 pl.BlockSpec((tm, D), lambda i: (i, 0))]
    return pl.pallas_call(
        functools.partial(_outproj_kernel, n_parts=len(lhs_parts)),
        grid=(T // tm,),
        in_specs=in_specs,
        out_specs=pl.BlockSpec((tm, D), lambda i: (i, 0)),
        out_shape=jax.ShapeDtypeStruct((T, D), F32),
        compiler_params=_cparams("parallel"),
        name="outproj",
    )(*lhs_parts, w_bf, x)


def _final_norm_kernel(x_ref, w_ref, o_ref, *, n_skip, tr):
    t = pl.program_id(1)
    r = pl.multiple_of(n_skip + t * tr, 8)
    xf = x_ref[0, pl.ds(r, tr), :]
    ms = jnp.mean(xf * xf, axis=-1, keepdims=True)
    o_ref[0] = xf * lax.rsqrt(ms + EPS) * w_ref[...]


def _final_norm(x, w, n_skip):
    B, L, D = x.shape
    Lo = L - n_skip
    tr = _row_tile(Lo, 512, 8)
    return pl.pallas_call(
        functools.partial(_final_norm_kernel, n_skip=n_skip, tr=tr),
        grid=(B, Lo // tr),
        in_specs=[pl.BlockSpec((1, L, D), lambda b, t: (b, 0, 0)),
                  pl.BlockSpec((1, D), lambda b, t: (0, 0))],
        out_specs=pl.BlockSpec((1, tr, D), lambda b, t: (b, t, 0)),
        out_shape=jax.ShapeDtypeStruct((B, Lo, D), F32),
        compiler_params=_cparams("parallel", "arbitrary"),
        name="final_norm",
    )(x, w.reshape(1, D))


def _conv_rows(win, cw, n):
    kw = cw.shape[0]
    out = win[8:8 + n] * cw[kw - 1:kw, :]
    for tau in range(1, kw):
        out = out + pltpu.roll(win, tau, 0)[8:8 + n] * cw[kw - 1 - tau:kw - tau, :]
    return out


def _gdn_prompt_kernel(alog_ref, dtb_ref, q_ref, k_ref, v_ref, cwq_ref, cwk_ref, cwv_ref,
                       braw_ref, araw_ref, gnw_ref, z_ref, o_ref, s_ref, S_scr,
                       *, n_meta, n_chunks, dk, hg):
    h0 = pl.program_id(1) * hg
    C = GDN_CHUNK
    a_coefs = [-jnp.exp(jnp.full((1, 1), alog_ref[h0 + hh], F32)) for hh in range(hg)]
    dt_bs = [dtb_ref[h0 + hh] for hh in range(hg)]
    ii = lax.broadcasted_iota(jnp.int32, (C, C), 0)
    jj = lax.broadcasted_iota(jnp.int32, (C, C), 1)
    eye = (ii == jj).astype(F32)
    gnw = gnw_ref[...]

    def load_win(ref, r0, first, lanes):
        if first:
            return jnp.concatenate([jnp.zeros((8, dk), F32), ref[0, 0:C, lanes]], axis=0)
        return ref[0, pl.ds(pl.multiple_of(r0 - 8, 8), C + 8), lanes]

    def solve_stage(probs):
        qs, ks, vs, gates = [], [], [], []
        for hh, r0, crow, first in probs:
            lanes = slice(hh * dk, (hh + 1) * dk)
            q = _silu(_conv_rows(load_win(q_ref, r0, first, lanes), cwq_ref[:, lanes], C))
            k = _silu(_conv_rows(load_win(k_ref, r0, first, lanes), cwk_ref[:, lanes], C))
            v = _silu(_conv_rows(load_win(v_ref, r0, first, lanes), cwv_ref[:, lanes], C))
            qs.append(q * (lax.rsqrt(jnp.sum(q * q, axis=-1, keepdims=True) + EPS) * (dk ** -0.5)))
            ks.append(k * lax.rsqrt(jnp.sum(k * k, axis=-1, keepdims=True) + EPS))
            vs.append(v)
            beta_row = _sigmoid(braw_ref[0, hh, pl.ds(crow, 1), :])
            g_row = a_coefs[hh] * _softplus(araw_ref[0, hh, pl.ds(crow, 1), :] + dt_bs[hh])
            if first:
                valid = lax.broadcasted_iota(jnp.int32, (1, C), 1) < n_meta
                beta_row = jnp.where(valid, beta_row, 0.0)
                g_row = jnp.where(valid, g_row, 0.0)
            g_rb = jnp.broadcast_to(g_row, (C, C))
            gc_col = jnp.sum(jnp.where(jj <= ii, g_rb, 0.0), axis=1, keepdims=True)
            g_col = jnp.sum(jnp.where(jj == ii, g_rb, 0.0), axis=1, keepdims=True)
            gc_row = jnp.sum(jnp.where(ii <= jj, jnp.broadcast_to(g_col, (C, C)), 0.0),
                             axis=0, keepdims=True)
            beta_col = jnp.sum(jnp.where(jj == ii, jnp.broadcast_to(beta_row, (C, C)), 0.0),
                               axis=1, keepdims=True)
            g_last = jnp.sum(g_row, axis=1, keepdims=True)
            decay = jnp.exp(jnp.where(ii >= jj, gc_col - gc_row, NEG_BIG))
            gates.append((beta_col, gc_col, g_last, decay, jnp.exp(gc_col)))
        n = len(probs)
        kbs = [ks[p] * gates[p][0] for p in range(n)]
        Xs = [_dot_nt(jnp.concatenate([qs[p], kbs[p]], axis=0).astype(BF16), ks[p].astype(BF16))
              for p in range(n)]
        qks = [Xs[p][:C] * gates[p][3] for p in range(n)]
        Ns = [-jnp.where(ii > jj, Xs[p][C:] * gates[p][3], 0.0) for p in range(n)]
        Tinvs = [eye + N for N in Ns]
        for _ in range(max(1, int(math.ceil(math.log2(C))) - 1)):
            Ns = [_dot(N.astype(BF16), N.astype(BF16)) for N in Ns]
            Tinvs = [Tinvs[p] + _dot(Tinvs[p].astype(BF16), Ns[p].astype(BF16)) for p in range(n)]
        sols = [_dot(Tinvs[p].astype(BF16),
                     jnp.concatenate([vs[p] * gates[p][0], kbs[p] * gates[p][4]], axis=1).astype(BF16))
                for p in range(n)]
        dv = vs[0].shape[1]
        return [(sols[p][:, :dv], sols[p][:, dv:], qks[p], qs[p] * gates[p][4],
                 ks[p] * jnp.exp(gates[p][2] - gates[p][1]), jnp.exp(gates[p][2])) for p in range(n)]

    def state_stage(items, r0, n_store):
        Ss = [S_scr[hh] for hh in range(hg)]
        WQs = [_dot(jnp.concatenate([items[hh][1], items[hh][3]], axis=0).astype(BF16), Ss[hh].astype(BF16))
               for hh in range(hg)]
        v_news = [items[hh][0] - WQs[hh][:C] for hh in range(hg)]
        os_ = [WQs[hh][C:] + _dot(items[hh][2].astype(BF16), v_news[hh].astype(BF16)) for hh in range(hg)]
        S_news = [Ss[hh] * items[hh][5] + _dot_tn(items[hh][4].astype(BF16), v_news[hh].astype(BF16))
                  for hh in range(hg)]
        for hh in range(hg):
            lanes = slice(hh * dk, (hh + 1) * dk)
            S_scr[hh] = S_news[hh]
            o = os_[hh]
            on = o * lax.rsqrt(jnp.mean(o * o, axis=-1, keepdims=True) + EPS) * gnw
            o_ref[0, pl.ds(r0, n_store), lanes] = (
                on[:n_store] * _silu(z_ref[0, pl.ds(r0, n_store), lanes])).astype(o_ref.dtype)

    S_scr[...] = jnp.zeros(S_scr.shape, F32)
    state_stage(solve_stage([(hh, 0, 0, True) for hh in range(hg)]), 0, n_meta)

    cb = GDN_CHUNKS_PER_ITER if n_chunks % GDN_CHUNKS_PER_ITER == 0 else 1

    def body(t, carry):
        r0s = [pl.multiple_of(n_meta + (t * cb + cc) * C, 16) for cc in range(cb)]
        items = solve_stage([(hh, r0s[cc], t * cb + cc + 1, False) for cc in range(cb) for hh in range(hg)])
        for cc in range(cb):
            state_stage(items[cc * hg:(cc + 1) * hg], r0s[cc], C)
        return carry

    lax.fori_loop(0, n_chunks // cb, body, 0)
    s_ref[0] = S_scr[...]


def _gdn_prompt(P, gates_c, conv_w, a_log, dt_bias, gn_w, n_meta, H, dk, dv, off_z):
    B, L, _ = P.shape
    n_chunks = (L - n_meta) // GDN_CHUNK
    hg = GDN_HEADS_PER_STEP if H % GDN_HEADS_PER_STEP == 0 else 1
    ng = H // hg
    zb0 = off_z // (hg * dv)
    cw = conv_w.shape[0]
    spec_col = lambda off: pl.BlockSpec((1, L, hg * dk), lambda b, g: (b, 0, off + g))
    spec_cw = lambda off: pl.BlockSpec((cw, hg * dk), lambda b, g: (0, off + g))
    smem = pl.BlockSpec(memory_space=pltpu.SMEM)
    nc1 = gates_c.shape[2]
    return pl.pallas_call(
        functools.partial(_gdn_prompt_kernel, n_meta=n_meta, n_chunks=n_chunks, dk=dk, hg=hg),
        grid=(B, ng),
        in_specs=[smem, smem, spec_col(0), spec_col(ng), spec_col(2 * ng),
                  spec_cw(0), spec_cw(ng), spec_cw(2 * ng),
                  pl.BlockSpec((1, hg, nc1, GDN_CHUNK), lambda b, g: (b, g, 0, 0)),
                  pl.BlockSpec((1, hg, nc1, GDN_CHUNK), lambda b, g: (b, ng + g, 0, 0)),
                  pl.BlockSpec((1, dv), lambda b, g: (0, 0)),
                  pl.BlockSpec((1, L, hg * dv), lambda b, g: (b, 0, zb0 + g))],
        out_specs=[pl.BlockSpec((1, L, hg * dv), lambda b, g: (b, 0, g)),
                   pl.BlockSpec((1, hg, dk, dv), lambda b, g: (b, g, 0, 0))],
        out_shape=[jax.ShapeDtypeStruct((B, L, H * dv), BF16),
                   jax.ShapeDtypeStruct((B, H, dk, dv), F32)],
        scratch_shapes=[pltpu.VMEM((hg, dk, dv), F32)],
        compiler_params=_cparams("parallel", "arbitrary"),
        name="gdn_prompt",
    )(a_log, dt_bias, P, P, P, conv_w, conv_w, conv_w, gates_c, gates_c,
      gn_w.reshape(1, dv), P)


def _sb_prompt_kernel(bias_ref, q_ref, k_ref, v_ref, z_ref, o_ref, ko_ref, vo_ref,
                      *, n_meta, n_qb, qb, dh):
    h = pl.program_id(1)
    MB = SB_BLOCK
    bias = bias_ref[h]
    scale = dh ** -0.5

    def strict_lower(n):
        return (lax.broadcasted_iota(jnp.int32, (n, n), 0)
                > lax.broadcasted_iota(jnp.int32, (n, n), 1)).astype(BF16)

    gw_main = min(qb, SB_GROUP)
    U_main = strict_lower(gw_main)
    U_meta = U_main if gw_main == MB else strict_lower(MB)

    ko_ref[...] = k_ref[...]
    vo_ref[...] = v_ref[...]

    def tile(q_bf, ks, kw, mode, carry):
        acc, c = carry
        rows = q_bf.shape[0]
        kblk = k_ref[0, pl.ds(ks, kw), :].astype(BF16)
        vblk = v_ref[0, pl.ds(ks, kw), :].astype(BF16)
        z = _dot_nt(q_bf, kblk) * scale + bias
        sp = jnp.maximum(z, 0.0) + jnp.log(1.0 + jnp.exp(-jnp.abs(z)))
        if mode is None:
            vis = None
            l1m = -sp
        else:
            col = lax.broadcasted_iota(jnp.int32, (rows, kw), 1)
            vis = (col < lax.broadcasted_iota(jnp.int32, (rows, kw), 0)) if mode == "diag" else (col < n_meta)
            l1m = jnp.where(vis, -sp, 0.0)
        ls = z - sp
        gw = min(kw, SB_GROUP)
        U = U_main if gw == gw_main else U_meta
        l1m_bf = l1m.astype(BF16)
        a_parts = [None] * (kw // gw)
        for g in range(kw // gw - 1, -1, -1):
            sl = slice(g * gw, (g + 1) * gw)
            E = _dot(l1m_bf[:, sl], U)
            a = jnp.exp(ls[:, sl] + E + c)
            a_parts[g] = a if vis is None else jnp.where(vis[:, sl], a, 0.0)
            c = c + (E[:, 0:1] + l1m[:, g * gw:g * gw + 1])
        a_all = a_parts[0] if len(a_parts) == 1 else jnp.concatenate(a_parts, axis=1)
        return acc + _dot(a_all.astype(BF16), vblk), c

    def zero(rows):
        return (jnp.zeros((rows, dh), F32), jnp.zeros((rows, 1), F32))

    acc0, _ = tile(q_ref[0, 0:MB, :].astype(BF16), 0, MB, "diag", zero(MB))
    o_ref[0, 0:n_meta, :] = (acc0[:n_meta] * _silu(z_ref[0, 0:n_meta, :])).astype(o_ref.dtype)

    def qblock(i, carry):
        qs = pl.multiple_of(n_meta + i * qb, 16)
        q_bf = q_ref[0, pl.ds(qs, qb), :].astype(BF16)
        st = tile(q_bf, qs, qb, "diag", zero(qb))

        def kstep(t, st):
            ks = pl.multiple_of(n_meta + (i - 1 - t) * qb, 16)
            return tile(q_bf, ks, qb, None, st)

        st = lax.fori_loop(0, i, kstep, st)
        acc, _ = tile(q_bf, 0, MB, "meta", st)
        o_ref[0, pl.ds(qs, qb), :] = (acc * _silu(z_ref[0, pl.ds(qs, qb), :])).astype(o_ref.dtype)
        return carry

    lax.fori_loop(0, n_qb, qblock, 0)


def _sb_prompt(P, sb_bias, n_meta, H, dh, off_q, off_z):
    B, L, _ = P.shape
    qb = next(c for c in (512, 256, 128) if (L - n_meta) % c == 0)
    n_qb = (L - n_meta) // qb
    qb0, zb0 = off_q // dh, off_z // dh
    col = lambda off: pl.BlockSpec((1, L, dh), lambda b, h: (b, 0, off + h))
    return pl.pallas_call(
        functools.partial(_sb_prompt_kernel, n_meta=n_meta, n_qb=n_qb, qb=qb, dh=dh),
        grid=(B, H),
        in_specs=[pl.BlockSpec(memory_space=pltpu.SMEM),
                  col(qb0), col(qb0 + H), col(qb0 + 2 * H), col(zb0)],
        out_specs=[col(0), col(0), col(0)],
        out_shape=[jax.ShapeDtypeStruct((B, L, H * dh), BF16),
                   jax.ShapeDtypeStruct((B, L, H * dh), F32),
                   jax.ShapeDtypeStruct((B, L, H * dh), F32)],
        compiler_params=_cparams("parallel", "arbitrary"),
        name="sb_prompt",
    )(sb_bias, P, P, P, P)


def _lru_gates(xc, w_ref, ba, bx, ls_lam):
    bw = xc.shape[1]
    pre = _dot(xc.astype(BF16), w_ref[0])
    r = _sigmoid(pre[:, :bw] + ba)
    i = _sigmoid(pre[:, bw:] + bx)
    log_a = RG_C * r * ls_lam
    a = jnp.exp(log_a)
    b = jnp.sqrt(1.0 - a * a) * (i * xc)
    return a, b


def _lru_prompt_kernel(x_ref, g_ref, cw_ref, cb_ref, w_ref, ba_ref, bx_ref, lam_ref,
                       y_ref, hl_ref, h_scr, *, n_meta, n_chunks):
    TC = LRU_CHUNK
    cw = cw_ref[...]
    cb = cb_ref[...]
    ba, bx = ba_ref[...], bx_ref[...]
    ls_lam = -_softplus(-lam_ref[...])
    bw = x_ref.shape[2]

    def scan_chunk(win, n, r0, h_prev):
        xc = _conv_rows(win, cw, n) + cb
        a, b = _lru_gates(xc, w_ref, ba, bx, ls_lam)
        k = 1
        while k < n:
            a_s = _shift_down(a, k, 1.0)
            b_s = _shift_down(b, k, 0.0)
            b = a * b_s + b
            a = a * a_s
            k *= 2
        hrows = a * h_prev + b
        y_ref[0, pl.ds(r0, n), :] = (hrows * _silu(g_ref[0, pl.ds(r0, n), :])).astype(y_ref.dtype)
        return hrows[n - 1:n, :]

    win0 = jnp.concatenate([jnp.zeros((8, bw), F32), x_ref[0, 0:n_meta, :]], axis=0)
    h_scr[...] = scan_chunk(win0, n_meta, 0, jnp.zeros((1, bw), F32))

    def body(c, carry):
        r0 = pl.multiple_of(n_meta + c * TC, 16)
        win = x_ref[0, pl.ds(pl.multiple_of(r0 - 8, 8), TC + 8), :]
        h_scr[...] = scan_chunk(win, TC, r0, h_scr[...])
        return carry

    lax.fori_loop(0, n_chunks, body, 0)
    hl_ref[0] = h_scr[...]


def _lru_prompt(P, conv_w, conv_b, w_ax_bf, b_a, b_x, lam, n_meta, w_c):
    B, L, _ = P.shape
    hc, bw, _ = w_ax_bf.shape
    n_chunks = (L - n_meta) // LRU_CHUNK
    kw = conv_w.shape[0]
    vec = lambda a: a.reshape(1, w_c)
    vspec = pl.BlockSpec((1, bw), lambda b, h: (0, h))
    return pl.pallas_call(
        functools.partial(_lru_prompt_kernel, n_meta=n_meta, n_chunks=n_chunks),
        grid=(B, hc),
        in_specs=[pl.BlockSpec((1, L, bw), lambda b, h: (b, 0, h)),
                  pl.BlockSpec((1, L, bw), lambda b, h: (b, 0, hc + h)),
                  pl.BlockSpec((kw, bw), lambda b, h: (0, h)),
                  vspec,
                  pl.BlockSpec((1, bw, 2 * bw), lambda b, h: (h, 0, 0)),
                  vspec, vspec, vspec],
        out_specs=[pl.BlockSpec((1, L, bw), lambda b, h: (b, 0, h)),
                   pl.BlockSpec((1, 1, bw), lambda b, h: (b, 0, h))],
        out_shape=[jax.ShapeDtypeStruct((B, L, w_c), BF16),
                   jax.ShapeDtypeStruct((B, 1, w_c), F32)],
        scratch_shapes=[pltpu.VMEM((1, bw), F32)],
        compiler_params=_cparams("parallel", "arbitrary"),
        name="lru_prompt",
    )(P, P, conv_w, vec(conv_b), w_ax_bf, vec(b_a), vec(b_x), vec(lam))


def _gdn_sample_kernel(alog_ref, dtb_ref, u_ref, buf_ref, cw_ref, gates_ref, gnw_ref, z_ref, s_ref,
                       o_ref, so_ref, *, H, dk, dv):
    kw = cw_ref.shape[0]
    xc = u_ref[0] * cw_ref[kw - 1:kw, :]
    for t in range(kw - 1):
        xc = xc + buf_ref[0, t:t + 1, :] * cw_ref[t:t + 1, :]
    xc = _silu(xc)
    gates = gates_ref[0]
    gnw = gnw_ref[...]
    row8 = lax.broadcasted_iota(jnp.int32, (8, dk), 0)
    di = lax.broadcasted_iota(jnp.int32, (dk, dk), 0)
    dj = lax.broadcasted_iota(jnp.int32, (dk, dk), 1)
    for h in range(H):
        q = xc[:, h * dk:(h + 1) * dk]
        k = xc[:, (H + h) * dk:(H + h + 1) * dk]
        v = xc[:, 2 * H * dk + h * dv:2 * H * dk + (h + 1) * dv]
        q = q * (lax.rsqrt(jnp.sum(q * q, axis=-1, keepdims=True) + EPS) * (dk ** -0.5))
        k = k * lax.rsqrt(jnp.sum(k * k, axis=-1, keepdims=True) + EPS)
        beta = _sigmoid(gates[:, h:h + 1])
        g = -jnp.exp(jnp.full((1, 1), alog_ref[h], F32)) * _softplus(gates[:, H + h:H + h + 1] + dtb_ref[h])
        eg = jnp.exp(g)
        S = s_ref[0, h]
        kq = jnp.where(row8 == 0, jnp.broadcast_to(k, (8, dk)),
                       jnp.where(row8 == 1, jnp.broadcast_to(q * eg, (8, dk)), 0.0))
        R = _dot(kq.astype(BF16), S.astype(BF16))
        v_new = beta * (v - eg * R[0:1])
        o = R[1:2] + jnp.sum(q * k, axis=-1, keepdims=True) * v_new
        k_col = jnp.sum(jnp.where(di == dj, jnp.broadcast_to(k, (dk, dk)), 0.0),
                        axis=1, keepdims=True)
        so_ref[0, h] = S * eg + k_col * v_new
        on = o * lax.rsqrt(jnp.mean(o * o, axis=-1, keepdims=True) + EPS) * gnw
        o_ref[0, :, h * dv:(h + 1) * dv] = (on * _silu(z_ref[0, :, h * dv:(h + 1) * dv])).astype(o_ref.dtype)


def _gdn_sample(u, buf, conv_w, gates, a_log, dt_bias, gn_w, z_a, S, H, dk, dv):
    Bd, C = u.shape
    kw = conv_w.shape[0]
    smem = pl.BlockSpec(memory_space=pltpu.SMEM)
    row = lambda n: pl.BlockSpec((1, 1, n), lambda b: (b, 0, 0))
    o, s_new = pl.pallas_call(
        functools.partial(_gdn_sample_kernel, H=H, dk=dk, dv=dv),
        grid=(Bd,),
        in_specs=[smem, smem, row(C),
                  pl.BlockSpec((1, kw - 1, C), lambda b: (b, 0, 0)),
                  pl.BlockSpec((kw, C), lambda b: (0, 0)),
                  row(2 * H),
                  pl.BlockSpec((1, dv), lambda b: (0, 0)),
                  row(H * dv),
                  pl.BlockSpec((1, H, dk, dv), lambda b: (b, 0, 0, 0))],
        out_specs=[row(H * dv), pl.BlockSpec((1, H, dk, dv), lambda b: (b, 0, 0, 0))],
        out_shape=[jax.ShapeDtypeStruct((Bd, 1, H * dv), BF16),
                   jax.ShapeDtypeStruct((Bd, H, dk, dv), F32)],
        compiler_params=_cparams("parallel"),
        name="gdn_sample",
    )(a_log, dt_bias, u.reshape(Bd, 1, C), buf, conv_w, gates.reshape(Bd, 1, 2 * H),
      gn_w.reshape(1, dv), z_a.reshape(Bd, 1, H * dv), S)
    return o.reshape(Bd, H * dv), s_new


def _sb_sample_kernel(pt_ref, q_ref, bias_ref, z_ref, *refs, H, dh, page, pps, n_steps):
    k_refs = refs[:pps]
    v_refs = refs[pps:2 * pps]
    o_ref, acc_scr, c_scr = refs[2 * pps:]
    p = pl.program_id(1)
    LW = 128
    nv = (page * H) // LW

    @pl.when(p == 0)
    def _():
        acc_scr[...] = jnp.zeros(acc_scr.shape, F32)
        c_scr[...] = jnp.zeros(c_scr.shape, F32)

    q_bf = (q_ref[0] * (dh ** -0.5)).astype(BF16)
    bias = bias_ref[...]
    lane = lax.broadcasted_iota(jnp.int32, (H, LW), 1)
    sub = lax.broadcasted_iota(jnp.int32, (H, LW), 0)
    valid = (lane % H) == sub
    ui = lax.broadcasted_iota(jnp.int32, (LW, LW), 0)
    uj = lax.broadcasted_iota(jnp.int32, (LW, LW), 1)
    U = (ui > uj).astype(BF16)

    ls_all, l1m_all = [], []
    for s in range(pps):
        kp = k_refs[s][0, 0].astype(BF16)
        z = _dot_nt(q_bf, kp) + bias
        sp = jnp.maximum(z, 0.0) + jnp.log(1.0 + jnp.exp(-jnp.abs(z)))
        ls_all.append(z - sp)
        l1m_all += [jnp.where(valid, -sp[:, i * LW:(i + 1) * LW], 0.0) for i in range(nv)]
    L = jnp.concatenate(l1m_all, axis=0)
    hi = L.astype(BF16)
    lo = (L - hi.astype(F32)).astype(BF16)
    n_l = pps * nv * H
    E2 = _dot(jnp.concatenate([hi, lo], axis=0), U)
    E = E2[:n_l] + E2[n_l:]
    tot = E[:, 0:1] + L[:, 0:1]
    acc = acc_scr[...]
    c = c_scr[...]
    for s in range(pps):
        a_parts = [None] * nv
        for i in range(nv - 1, -1, -1):
            g = s * nv + i
            loga = ls_all[s][:, i * LW:(i + 1) * LW] + E[g * H:(g + 1) * H] + c
            a_parts[i] = jnp.where(valid, jnp.exp(loga), 0.0)
            c = c + tot[g * H:(g + 1) * H]
        a = jnp.concatenate(a_parts, axis=1).astype(BF16)
        acc = acc + _dot(a, v_refs[s][0, 0].astype(BF16))
    acc_scr[...] = acc
    c_scr[...] = c

    @pl.when(p == n_steps - 1)
    def _():
        o_ref[0] = (acc * _silu(z_ref[0])).astype(o_ref.dtype)


def _sb_sample(q, z_b, sb_bias, cache_k, cache_v, page_table, layer):
    Bd, H, dh = q.shape
    _, n_phys, page, _, _ = cache_k.shape
    n_pages = page_table.shape[1]
    pps = SB_PAGES_PER_STEP if n_pages % SB_PAGES_PER_STEP == 0 else 1
    n_steps = n_pages // pps
    ck = cache_k.reshape(cache_k.shape[0], n_phys, page * H, dh)
    cv = cache_v.reshape(cache_v.shape[0], n_phys, page * H, dh)

    def page_spec(s):
        def imap(b, p, pt):
            return (layer, pt[b * n_pages + (n_pages - 1 - (p * pps + s))], 0, 0)
        return pl.BlockSpec((1, 1, page * H, dh), imap)

    hd = pl.BlockSpec((1, H, dh), lambda b, p, pt: (b, 0, 0))
    grid_spec = pltpu.PrefetchScalarGridSpec(
        num_scalar_prefetch=1,
        grid=(Bd, n_steps),
        in_specs=[hd, pl.BlockSpec((H, 1), lambda b, p, pt: (0, 0)), hd]
                 + [page_spec(s) for s in range(pps)] * 2,
        out_specs=hd,
        scratch_shapes=[pltpu.VMEM((H, dh), F32), pltpu.VMEM((H, 1), F32)],
    )
    return pl.pallas_call(
        functools.partial(_sb_sample_kernel, H=H, dh=dh, page=page, pps=pps, n_steps=n_steps),
        grid_spec=grid_spec,
        out_shape=jax.ShapeDtypeStruct((Bd, H, dh), BF16),
        compiler_params=_cparams("parallel", "arbitrary"),
        name="sb_sample",
    )(page_table.reshape(-1), q, sb_bias.reshape(H, 1), z_b, *([ck] * pps), *([cv] * pps))


def _lru_sample_kernel(x_ref, g_ref, buf_ref, cw_ref, cb_ref, w_ref, ba_ref, bx_ref, lam_ref, h0_ref,
                       y_ref, h_ref):
    kw = cw_ref.shape[0]
    xc = x_ref[...] * cw_ref[kw - 1:kw, :] + cb_ref[...]
    for t in range(kw - 1):
        xc = xc + buf_ref[t] * cw_ref[t:t + 1, :]
    ls_lam = -_softplus(-lam_ref[...])
    a, b = _lru_gates(xc, w_ref, ba_ref[...], bx_ref[...], ls_lam)
    hn = a * h0_ref[...] + b
    h_ref[...] = hn
    y_ref[...] = (hn * _silu(g_ref[...])).astype(y_ref.dtype)


def _lru_sample(P, buf_t, conv_w, conv_b, w_ax_bf, b_a, b_x, lam, h0, w_c):
    Bd = P.shape[0]
    hc, bw, _ = w_ax_bf.shape
    kw = conv_w.shape[0]
    vec = lambda a: a.reshape(1, w_c)
    vspec = pl.BlockSpec((1, bw), lambda h: (0, h))
    mspec = lambda off: pl.BlockSpec((Bd, bw), lambda h: (0, off + h))
    return pl.pallas_call(
        _lru_sample_kernel,
        grid=(hc,),
        in_specs=[mspec(0), mspec(hc),
                  pl.BlockSpec((kw - 1, Bd, bw), lambda h: (0, 0, h)),
                  pl.BlockSpec((kw, bw), lambda h: (0, h)),
                  vspec,
                  pl.BlockSpec((1, bw, 2 * bw), lambda h: (h, 0, 0)),
                  vspec, vspec, vspec, mspec(0)],
        out_specs=[mspec(0), mspec(0)],
        out_shape=[jax.ShapeDtypeStruct((Bd, w_c), BF16), jax.ShapeDtypeStruct((Bd, w_c), F32)],
        compiler_params=_cparams("parallel"),
        name="lru_sample",
    )(P, P, buf_t, conv_w, vec(conv_b), w_ax_bf, vec(b_a), vec(b_x), vec(lam), h0)


def kernel(x_prompt, x_sample, cache_sb_k, cache_sb_v, state_gdn, state_gdn_conv, state_lru,
           state_lru_conv, page_table, meta_tokens, norm_w, final_norm_w, w_in_even, gdn_conv_w,
           gdn_a_log, gdn_dt_bias, gdn_norm_w, sb_bias, w_out_even, w_in_odd, lru_conv_w, lru_conv_b,
           lru_w_a, lru_b_a, lru_w_x, lru_b_x, lru_lambda, w_out_odd):
    B, seq, D = x_prompt.shape
    Bd = x_sample.shape[0]
    assert x_sample.shape[1] == 1
    n_meta = meta_tokens.shape[0]
    L = n_meta + seq
    depth = norm_w.shape[0]
    _, _, H_a, dk, dv = state_gdn.shape
    _, _, _, H_b, dh = cache_sb_k.shape
    w_a_tot, w_b_tot = H_a * dv, H_b * dh
    qkv_a = 2 * H_a * dk + H_a * dv
    off_qkv_b = qkv_a
    off_z_a = off_qkv_b + 3 * w_b_tot
    off_z_b = off_z_a + w_a_tot
    off_gate = off_z_b + w_b_tot
    w_c = state_lru.shape[2]
    assert dk == dv == dh == 128 and n_meta % 16 == 0 and n_meta <= GDN_CHUNK
    assert seq % LRU_CHUNK == 0 and seq % SB_BLOCK == 0 and seq % GDN_CHUNK == 0
    assert w_in_even.shape[2] == off_gate + 2 * H_a and lru_w_a.shape[2] == 128

    meta = jnp.broadcast_to(meta_tokens.astype(x_prompt.dtype)[None], (B, n_meta, D))
    xp = jnp.concatenate([meta, x_prompt], axis=1).reshape(B * L, D)
    xs = x_sample.reshape(Bd, D)

    n_chunks = seq // GDN_CHUNK
    pk, pv, pS, pcg, ph, pcl = [], [], [], [], [], []
    sk, sv, sS, scg, sh, scl = [], [], [], [], [], []
    for layer in range(depth):
        j = layer // 2
        if layer % 2 == 0:
            w_bf = w_in_even[j].astype(BF16)
            w_gate_bf = w_bf[:, off_gate:]
            w_out_bf = w_out_even[j].astype(BF16)
            P, gates = _inproj(xp, norm_w[layer], w_bf, off_gate, w_gate_bf)
            P3 = P.reshape(B, L, off_gate)
            g3 = jnp.swapaxes(gates.reshape(B, L, 2 * H_a), 1, 2)
            g_meta = jnp.pad(g3[:, :, :n_meta], ((0, 0), (0, 0), (0, GDN_CHUNK - n_meta)))
            gates_c = jnp.concatenate(
                [g_meta[:, :, None, :], g3[:, :, n_meta:].reshape(B, 2 * H_a, n_chunks, GDN_CHUNK)], axis=2)
            la, S_p = _gdn_prompt(P3, gates_c, gdn_conv_w[j], gdn_a_log[j], gdn_dt_bias[j], gdn_norm_w[j],
                                  n_meta, H_a, dk, dv, off_z_a)
            lb, k_p, v_p = _sb_prompt(P3, sb_bias[j], n_meta, H_b, dh, off_qkv_b, off_z_b)
            xp = _outproj([la.reshape(B * L, w_a_tot), lb.reshape(B * L, w_b_tot)], w_out_bf, xp)
            pk.append(k_p.reshape(B, L, H_b, dh)); pv.append(v_p.reshape(B, L, H_b, dh))
            pS.append(S_p); pcg.append(P3[:, L - (gdn_conv_w.shape[1] - 1):, :qkv_a])
            Ps, gs = _inproj(xs, norm_w[layer], w_bf, off_gate, w_gate_bf)
            u = Ps[:, :qkv_a]
            las, S_s = _gdn_sample(u, state_gdn_conv[j], gdn_conv_w[j], gs, gdn_a_log[j], gdn_dt_bias[j],
                                   gdn_norm_w[j], Ps[:, off_z_a:off_z_a + w_a_tot], state_gdn[j], H_a, dk, dv)
            q_s = Ps[:, off_qkv_b:off_qkv_b + w_b_tot].reshape(Bd, H_b, dh)
            k_s = Ps[:, off_qkv_b + w_b_tot:off_qkv_b + 2 * w_b_tot]
            v_s = Ps[:, off_qkv_b + 2 * w_b_tot:off_qkv_b + 3 * w_b_tot]
            z_s = Ps[:, off_z_b:off_z_b + w_b_tot].reshape(Bd, H_b, dh)
            lbs = _sb_sample(q_s, z_s, sb_bias[j], cache_sb_k, cache_sb_v, page_table, j)
            xs = _outproj([las, lbs.reshape(Bd, w_b_tot)], w_out_bf, xs)
            sk.append(k_s.reshape(Bd, 1, H_b, dh)); sv.append(v_s.reshape(Bd, 1, H_b, dh))
            sS.append(S_s)
            scg.append(jnp.concatenate([state_gdn_conv[j][:, 1:], u[:, None, :]], axis=1))
        else:
            w_bf = w_in_odd[j].astype(BF16)
            w_out_bf = w_out_odd[j].astype(BF16)
            w_ax_bf = jnp.concatenate([lru_w_a[j], lru_w_x[j]], axis=-1).astype(BF16)
            lru_args = (lru_conv_w[j], lru_conv_b[j], w_ax_bf, lru_b_a[j], lru_b_x[j], lru_lambda[j])
            P = _inproj(xp, norm_w[layer], w_bf, 2 * w_c)
            P3 = P.reshape(B, L, 2 * w_c)
            y, h_last = _lru_prompt(P3, *lru_args, n_meta, w_c)
            xp = _outproj([y.reshape(B * L, w_c)], w_out_bf, xp)
            ph.append(h_last.reshape(B, w_c)); pcl.append(P3[:, L - (lru_conv_w.shape[1] - 1):, :w_c])
            Ps = _inproj(xs, norm_w[layer], w_bf, 2 * w_c)
            ys, h_s = _lru_sample(Ps, jnp.swapaxes(state_lru_conv[j], 0, 1), *lru_args, state_lru[j], w_c)
            xs = _outproj([ys], w_out_bf, xs)
            sh.append(h_s)
            scl.append(jnp.concatenate([state_lru_conv[j][:, 1:], Ps[:, None, :w_c]], axis=1))

    y_prompt = _final_norm(xp.reshape(B, L, D), final_norm_w, n_meta)
    y_sample = _final_norm(xs.reshape(1, Bd, D), final_norm_w, 0).reshape(Bd, 1, D)
    st = jnp.stack
    return (y_prompt, y_sample, st(pk), st(pv), st(pS), st(pcg), st(ph), st(pcl),
            st(sk), st(sv), st(sS), st(scg), st(sh), st(scl))
```

```python
import functools
import math

import jax
import jax.numpy as jnp
from jax import lax
from jax.experimental import pallas as pl
from jax.experimental.pallas import tpu as pltpu

EPS = 1e-6
RG_C = 8.0
GDN_CHUNK = 64
SB_BLOCK = 128
SB_GROUP = 256
LRU_CHUNK = 256
GDN_HEADS_PER_STEP = 4
GDN_CHUNKS_PER_ITER = 4
SB_PAGES_PER_STEP = 8
F32 = jnp.float32
BF16 = jnp.bfloat16
V7X_VMEM_LIMIT_BYTES = 56 * 1024 * 1024
NEG_BIG = -1e30


def _cparams(*dims):
    return pltpu.CompilerParams(dimension_semantics=dims, vmem_limit_bytes=V7X_VMEM_LIMIT_BYTES)


def _row_tile(n, cap, mult=16):
    best = None
    for t in range(mult, min(n, cap) + 1, mult):
        if n % t == 0:
            best = t
    return best if best is not None else n


def _sigmoid(x):
    return 1.0 / (1.0 + jnp.exp(-x))


def _silu(x):
    return x * _sigmoid(x)


def _softplus(x):
    return jnp.maximum(x, 0.0) + jnp.log1p(jnp.exp(-jnp.abs(x)))


def _dot(a, b):
    return jnp.dot(a, b, preferred_element_type=F32)


def _dot_nt(a, b):
    return lax.dot_general(a, b, (((1,), (1,)), ((), ())), preferred_element_type=F32)


def _dot_tn(a, b):
    return lax.dot_general(a, b, (((0,), (0,)), ((), ())), preferred_element_type=F32)


def _shift_down(x, k, fill):
    if k % 8 == 0:
        return jnp.concatenate([jnp.full((k, x.shape[1]), fill, x.dtype), x[:x.shape[0] - k]], axis=0)
    rolled = pltpu.roll(x, k, 0)
    rows = lax.broadcasted_iota(jnp.int32, x.shape, 0)
    return jnp.where(rows >= k, rolled, fill)


def _rmsnorm_bf16(xf, nw):
    ms = jnp.mean(xf * xf, axis=-1, keepdims=True)
    return (xf * lax.rsqrt(ms + EPS) * nw).astype(BF16)


def _norm_kernel(x_ref, nw_ref, o_ref):
    o_ref[...] = _rmsnorm_bf16(x_ref[...], nw_ref[...])


def _norm(x, nw, tm_cap=128):
    T, D = x.shape
    tm = _row_tile(T, tm_cap)
    return pl.pallas_call(
        _norm_kernel,
        grid=(T // tm,),
        in_specs=[pl.BlockSpec((tm, D), lambda i: (i, 0)), pl.BlockSpec((1, D), lambda i: (0, 0))],
        out_specs=pl.BlockSpec((tm, D), lambda i: (i, 0)),
        out_shape=jax.ShapeDtypeStruct((T, D), BF16),
        compiler_params=_cparams("parallel"),
        name="norm",
    )(x, nw.reshape(1, D))


def _inproj_kernel(*refs, has_small):
    if has_small:
        x_ref, w_ref, ws_ref, o_ref, os_ref = refs

        @pl.when(pl.program_id(1) == 0)
        def _():
            os_ref[...] = _dot(x_ref[...], ws_ref[...])
    else:
        x_ref, w_ref, o_ref = refs
    o_ref[...] = _dot(x_ref[...], w_ref[...])


def _inproj(xn, w_bf, n_main, w_small_bf=None, tn=512, tm_cap=2064):
    T, D = xn.shape
    tm = _row_tile(T, tm_cap)
    has_small = w_small_bf is not None
    in_specs = [pl.BlockSpec((tm, D), lambda i, j: (i, 0)),
                pl.BlockSpec((D, tn), lambda i, j: (0, j))]
    out_shape = [jax.ShapeDtypeStruct((T, n_main), F32)]
    out_specs = [pl.BlockSpec((tm, tn), lambda i, j: (i, j))]
    args = [xn, w_bf]
    if has_small:
        ns = w_small_bf.shape[1]
        in_specs.append(pl.BlockSpec((D, ns), lambda i, j: (0, 0)))
        out_shape.append(jax.ShapeDtypeStruct((T, ns), F32))
        out_specs.append(pl.BlockSpec((tm, ns), lambda i, j: (i, 0)))
        args.append(w_small_bf)
    out = pl.pallas_call(
        functools.partial(_inproj_kernel, has_small=has_small),
        grid=(T // tm, n_main // tn),
        in_specs=in_specs, out_specs=out_specs, out_shape=out_shape,
        compiler_params=_cparams("parallel", "arbitrary"),
        name="inproj",
    )(*args)
    return out if has_small else out[0]


def _outproj_kernel(*refs, n_parts):
    lhs_refs = refs[:n_parts]
    w_ref, x_ref, nw_ref, o_ref, on_ref = refs[n_parts:]
    acc = x_ref[...]
    k0 = 0
    for lr in lhs_refs:
        kk = lr.shape[1]
        acc = acc + _dot(lr[...], w_ref[k0:k0 + kk, :])
        k0 += kk
    o_ref[...] = acc
    on_ref[...] = _rmsnorm_bf16(acc, nw_ref[...])


def _outproj(lhs_parts, w_bf, x, nw_next, tm_cap=384):
    T, D = x.shape
    tm = _row_tile(T, tm_cap)
    row = pl.BlockSpec((tm, D), lambda i: (i, 0))
    in_specs = [pl.BlockSpec((tm, p.shape[1]), lambda i: (i, 0)) for p in lhs_parts]
    in_specs += [pl.BlockSpec(w_bf.shape, lambda i: (0, 0)), row, pl.BlockSpec((1, D), lambda i: (0, 0))]
    return pl.pallas_call(
        functools.partial(_outproj_kernel, n_parts=len(lhs_parts)),
        grid=(T // tm,),
        in_specs=in_specs,
        out_specs=[row, row],
        out_shape=[jax.ShapeDtypeStruct((T, D), F32), jax.ShapeDtypeStruct((T, D), BF16)],
        compiler_params=_cparams("parallel"),
        name="outproj",
    )(*lhs_parts, w_bf, x, nw_next.reshape(1, D))


def _final_norm_kernel(x_ref, w_ref, o_ref, *, n_skip, tr):
    t = pl.program_id(1)
    r = pl.multiple_of(n_skip + t * tr, 8)
    xf = x_ref[0, pl.ds(r, tr), :]
    ms = jnp.mean(xf * xf, axis=-1, keepdims=True)
    o_ref[0] = xf * lax.rsqrt(ms + EPS) * w_ref[...]


def _final_norm(x, w, n_skip):
    B, L, D = x.shape
    Lo = L - n_skip
    tr = _row_tile(Lo, 512, 8)
    return pl.pallas_call(
        functools.partial(_final_norm_kernel, n_skip=n_skip, tr=tr),
        grid=(B, Lo // tr),
        in_specs=[pl.BlockSpec((1, L, D), lambda b, t: (b, 0, 0)),
                  pl.BlockSpec((1, D), lambda b, t: (0, 0))],
        out_specs=pl.BlockSpec((1, tr, D), lambda b, t: (b, t, 0)),
        out_shape=jax.ShapeDtypeStruct((B, Lo, D), F32),
        compiler_params=_cparams("parallel", "arbitrary"),
        name="final_norm",
    )(x, w.reshape(1, D))


def _conv_rows(win, cw, n):
    kw = cw.shape[0]
    out = win[8:8 + n] * cw[kw - 1:kw, :]
    for tau in range(1, kw):
        out = out + pltpu.roll(win, tau, 0)[8:8 + n] * cw[kw - 1 - tau:kw - tau, :]
    return out


def _conv_at(ref, cw, r0, n, lanes, scr):
    kw = cw.shape[0]
    scr[...] = ref[0, pl.ds(pl.multiple_of(r0 - 8, 8), n + 8), lanes]
    out = scr[8:8 + n, :] * cw[kw - 1:kw, :]
    for tau in range(1, kw):
        out = out + scr[8 - tau:8 - tau + n, :] * cw[kw - 1 - tau:kw - tau, :]
    return out


def _gdn_prompt_kernel(alog_ref, dtb_ref, q_ref, k_ref, v_ref, cwq_ref, cwk_ref, cwv_ref,
                       braw_ref, araw_ref, gnw_ref, z_ref, o_ref, s_ref, S_scr, win_scr,
                       *, n_meta, n_chunks, dk, hg):
    h0 = pl.program_id(1) * hg
    C = GDN_CHUNK
    a_coefs = [-jnp.exp(jnp.full((1, 1), alog_ref[h0 + hh], F32)) for hh in range(hg)]
    dt_bs = [dtb_ref[h0 + hh] for hh in range(hg)]
    ii = lax.broadcasted_iota(jnp.int32, (C, C), 0)
    jj = lax.broadcasted_iota(jnp.int32, (C, C), 1)
    eye = (ii == jj).astype(F32)
    gnw = gnw_ref[...]

    def conv(ref, cw_ref, r0, first, lanes, slot):
        if first:
            win = jnp.concatenate([jnp.zeros((8, dk), F32), ref[0, 0:C, lanes]], axis=0)
            return _conv_rows(win, cw_ref[:, lanes], C)
        return _conv_at(ref, cw_ref[:, lanes], r0, C, lanes, win_scr.at[slot])

    def solve_stage(probs):
        qs, ks, vs, gates = [], [], [], []
        for p, (hh, r0, crow, first) in enumerate(probs):
            lanes = slice(hh * dk, (hh + 1) * dk)
            q = _silu(conv(q_ref, cwq_ref, r0, first, lanes, 3 * p))
            k = _silu(conv(k_ref, cwk_ref, r0, first, lanes, 3 * p + 1))
            v = _silu(conv(v_ref, cwv_ref, r0, first, lanes, 3 * p + 2))
            qs.append(q * (lax.rsqrt(jnp.sum(q * q, axis=-1, keepdims=True) + EPS) * (dk ** -0.5)))
            ks.append(k * lax.rsqrt(jnp.sum(k * k, axis=-1, keepdims=True) + EPS))
            vs.append(v)
            beta_row = _sigmoid(braw_ref[0, hh, pl.ds(crow, 1), :])
            g_row = a_coefs[hh] * _softplus(araw_ref[0, hh, pl.ds(crow, 1), :] + dt_bs[hh])
            if first:
                valid = lax.broadcasted_iota(jnp.int32, (1, C), 1) < n_meta
                beta_row = jnp.where(valid, beta_row, 0.0)
                g_row = jnp.where(valid, g_row, 0.0)
            g_rb = jnp.broadcast_to(g_row, (C, C))
            gc_col = jnp.sum(jnp.where(jj <= ii, g_rb, 0.0), axis=1, keepdims=True)
            g_col = jnp.sum(jnp.where(jj == ii, g_rb, 0.0), axis=1, keepdims=True)
            gc_row = jnp.sum(jnp.where(ii <= jj, jnp.broadcast_to(g_col, (C, C)), 0.0),
                             axis=0, keepdims=True)
            beta_col = jnp.sum(jnp.where(jj == ii, jnp.broadcast_to(beta_row, (C, C)), 0.0),
                               axis=1, keepdims=True)
            g_last = jnp.sum(g_row, axis=1, keepdims=True)
            decay = jnp.exp(jnp.where(ii >= jj, gc_col - gc_row, NEG_BIG))
            gates.append((beta_col, gc_col, g_last, decay, jnp.exp(gc_col)))
        n = len(probs)
        kbs = [ks[p] * gates[p][0] for p in range(n)]
        Xs = [_dot_nt(jnp.concatenate([qs[p], kbs[p]], axis=0).astype(BF16), ks[p].astype(BF16))
              for p in range(n)]
        qks = [Xs[p][:C] * gates[p][3] for p in range(n)]
        Ns = [-jnp.where(ii > jj, Xs[p][C:] * gates[p][3], 0.0) for p in range(n)]
        Tinvs = [eye + N for N in Ns]
        for _ in range(max(1, int(math.ceil(math.log2(C))) - 1)):
            Ns = [_dot(N.astype(BF16), N.astype(BF16)) for N in Ns]
            Tinvs = [Tinvs[p] + _dot(Tinvs[p].astype(BF16), Ns[p].astype(BF16)) for p in range(n)]
        sols = [_dot(Tinvs[p].astype(BF16),
                     jnp.concatenate([vs[p] * gates[p][0], kbs[p] * gates[p][4]], axis=1).astype(BF16))
                for p in range(n)]
        dv = vs[0].shape[1]
        return [(sols[p][:, :dv], sols[p][:, dv:], qks[p], qs[p] * gates[p][4],
                 ks[p] * jnp.exp(gates[p][2] - gates[p][1]), jnp.exp(gates[p][2])) for p in range(n)]

    def state_stage(items, r0, n_store):
        Ss = [S_scr[hh] for hh in range(hg)]
        WQs = [_dot(jnp.concatenate([items[hh][1], items[hh][3]], axis=0).astype(BF16), Ss[hh].astype(BF16))
               for hh in range(hg)]
        v_news = [items[hh][0] - WQs[hh][:C] for hh in range(hg)]
        os_ = [WQs[hh][C:] + _dot(items[hh][2].astype(BF16), v_news[hh].astype(BF16)) for hh in range(hg)]
        S_news = [Ss[hh] * items[hh][5] + _dot_tn(items[hh][4].astype(BF16), v_news[hh].astype(BF16))
                  for hh in range(hg)]
        for hh in range(hg):
            lanes = slice(hh * dk, (hh + 1) * dk)
            S_scr[hh] = S_news[hh]
            o = os_[hh]
            on = o * lax.rsqrt(jnp.mean(o * o, axis=-1, keepdims=True) + EPS) * gnw
            o_ref[0, pl.ds(r0, n_store), lanes] = (
                on[:n_store] * _silu(z_ref[0, pl.ds(r0, n_store), lanes])).astype(o_ref.dtype)

    S_scr[...] = jnp.zeros(S_scr.shape, F32)
    state_stage(solve_stage([(hh, 0, 0, True) for hh in range(hg)]), 0, n_meta)

    cb = GDN_CHUNKS_PER_ITER if n_chunks % GDN_CHUNKS_PER_ITER == 0 else 1

    def body(t, carry):
        r0s = [pl.multiple_of(n_meta + (t * cb + cc) * C, 16) for cc in range(cb)]
        items = solve_stage([(hh, r0s[cc], t * cb + cc + 1, False) for cc in range(cb) for hh in range(hg)])
        for cc in range(cb):
            state_stage(items[cc * hg:(cc + 1) * hg], r0s[cc], C)
        return carry

    lax.fori_loop(0, n_chunks // cb, body, 0)
    s_ref[0] = S_scr[...]


def _gdn_prompt(P, gates_c, conv_w, a_log, dt_bias, gn_w, n_meta, H, dk, dv, off_z):
    B, L, _ = P.shape
    n_chunks = (L - n_meta) // GDN_CHUNK
    hg = GDN_HEADS_PER_STEP if H % GDN_HEADS_PER_STEP == 0 else 1
    ng = H // hg
    zb0 = off_z // (hg * dv)
    cw = conv_w.shape[0]
    spec_col = lambda off: pl.BlockSpec((1, L, hg * dk), lambda b, g: (b, 0, off + g))
    spec_cw = lambda off: pl.BlockSpec((cw, hg * dk), lambda b, g: (0, off + g))
    smem = pl.BlockSpec(memory_space=pltpu.SMEM)
    nc1 = gates_c.shape[2]
    return pl.pallas_call(
        functools.partial(_gdn_prompt_kernel, n_meta=n_meta, n_chunks=n_chunks, dk=dk, hg=hg),
        grid=(B, ng),
        in_specs=[smem, smem, spec_col(0), spec_col(ng), spec_col(2 * ng),
                  spec_cw(0), spec_cw(ng), spec_cw(2 * ng),
                  pl.BlockSpec((1, hg, nc1, GDN_CHUNK), lambda b, g: (b, g, 0, 0)),
                  pl.BlockSpec((1, hg, nc1, GDN_CHUNK), lambda b, g: (b, ng + g, 0, 0)),
                  pl.BlockSpec((1, dv), lambda b, g: (0, 0)),
                  pl.BlockSpec((1, L, hg * dv), lambda b, g: (b, 0, zb0 + g))],
        out_specs=[pl.BlockSpec((1, L, hg * dv), lambda b, g: (b, 0, g)),
                   pl.BlockSpec((1, hg, dk, dv), lambda b, g: (b, g, 0, 0))],
        out_shape=[jax.ShapeDtypeStruct((B, L, H * dv), BF16),
                   jax.ShapeDtypeStruct((B, H, dk, dv), F32)],
        scratch_shapes=[pltpu.VMEM((hg, dk, dv), F32),
                        pltpu.VMEM((3 * hg * GDN_CHUNKS_PER_ITER, GDN_CHUNK + 8, dk), F32)],
        compiler_params=_cparams("parallel", "arbitrary"),
        name="gdn_prompt",
    )(a_log, dt_bias, P, P, P, conv_w, conv_w, conv_w, gates_c, gates_c,
      gn_w.reshape(1, dv), P)


def _sb_prompt_kernel(bias_ref, q_ref, k_ref, v_ref, z_ref, o_ref, ko_ref, vo_ref,
                      *, n_meta, n_qb, qb, dh):
    h = pl.program_id(1)
    MB = SB_BLOCK
    bias = bias_ref[h]
    scale = dh ** -0.5

    def strict_lower(n):
        return jnp.where(lax.broadcasted_iota(jnp.int32, (n, n), 0)
                         > lax.broadcasted_iota(jnp.int32, (n, n), 1), -1.0, 0.0).astype(BF16)

    gw_main = min(qb, SB_GROUP)
    U_main = strict_lower(gw_main)
    U_meta = U_main if gw_main == MB else strict_lower(MB)

    ko_ref[...] = k_ref[...]
    vo_ref[...] = v_ref[...]

    def tile(q_bf, ks, kw, mode, carry):
        acc, c = carry
        rows = q_bf.shape[0]
        kblk = k_ref[0, pl.ds(ks, kw), :].astype(BF16)
        vblk = v_ref[0, pl.ds(ks, kw), :].astype(BF16)
        z = _dot_nt(q_bf, kblk) + bias
        sp = jnp.maximum(z, 0.0) + jnp.log(1.0 + jnp.exp(-jnp.abs(z)))
        ls = z - sp
        if mode is None:
            vis = None
        else:
            col = lax.broadcasted_iota(jnp.int32, (rows, kw), 1)
            vis = (col < lax.broadcasted_iota(jnp.int32, (rows, kw), 0)) if mode == "diag" else (col < n_meta)
            sp = jnp.where(vis, sp, 0.0)
        gw = min(kw, SB_GROUP)
        U = U_main if gw == gw_main else U_meta
        sp_bf = sp.astype(BF16)
        a_parts = [None] * (kw // gw)
        for g in range(kw // gw - 1, -1, -1):
            sl = slice(g * gw, (g + 1) * gw)
            E = _dot(sp_bf[:, sl], U)
            a = jnp.exp(ls[:, sl] + E + c)
            a_parts[g] = a if vis is None else jnp.where(vis[:, sl], a, 0.0)
            c = c + (E[:, 0:1] - sp[:, g * gw:g * gw + 1])
        a_all = a_parts[0] if len(a_parts) == 1 else jnp.concatenate(a_parts, axis=1)
        return acc + _dot(a_all.astype(BF16), vblk), c

    def zero(rows):
        return (jnp.zeros((rows, dh), F32), jnp.zeros((rows, 1), F32))

    acc0, _ = tile((q_ref[0, 0:MB, :] * scale).astype(BF16), 0, MB, "diag", zero(MB))
    o_ref[0, 0:n_meta, :] = (acc0[:n_meta] * _silu(z_ref[0, 0:n_meta, :])).astype(o_ref.dtype)

    def qblock(i, carry):
        qs = pl.multiple_of(n_meta + i * qb, 16)
        q_bf = (q_ref[0, pl.ds(qs, qb), :] * scale).astype(BF16)
        st = tile(q_bf, qs, qb, "diag", zero(qb))

        def kstep(t, st):
            ks = pl.multiple_of(n_meta + (i - 1 - t) * qb, 16)
            return tile(q_bf, ks, qb, None, st)

        st = lax.fori_loop(0, i, kstep, st)
        acc, _ = tile(q_bf, 0, MB, "meta", st)
        o_ref[0, pl.ds(qs, qb), :] = (acc * _silu(z_ref[0, pl.ds(qs, qb), :])).astype(o_ref.dtype)
        return carry

    lax.fori_loop(0, n_qb, qblock, 0)


def _sb_prompt(P, sb_bias, n_meta, H, dh, off_q, off_z):
    B, L, _ = P.shape
    qb = next(c for c in (512, 256, 128) if (L - n_meta) % c == 0)
    n_qb = (L - n_meta) // qb
    qb0, zb0 = off_q // dh, off_z // dh
    col = lambda off: pl.BlockSpec((1, L, dh), lambda b, h: (b, 0, off + h))
    return pl.pallas_call(
        functools.partial(_sb_prompt_kernel, n_meta=n_meta, n_qb=n_qb, qb=qb, dh=dh),
        grid=(B, H),
        in_specs=[pl.BlockSpec(memory_space=pltpu.SMEM),
                  col(qb0), col(qb0 + H), col(qb0 + 2 * H), col(zb0)],
        out_specs=[col(0), col(0), col(0)],
        out_shape=[jax.ShapeDtypeStruct((B, L, H * dh), BF16),
                   jax.ShapeDtypeStruct((B, L, H * dh), F32),
                   jax.ShapeDtypeStruct((B, L, H * dh), F32)],
        compiler_params=_cparams("parallel", "arbitrary"),
        name="sb_prompt",
    )(sb_bias, P, P, P, P)


def _lru_gates(xc, w_ref, ba, bx, ls_lam):
    bw = xc.shape[1]
    pre = _dot(xc.astype(BF16), w_ref[0])
    r = _sigmoid(pre[:, :bw] + ba)
    i = _sigmoid(pre[:, bw:] + bx)
    log_a = RG_C * r * ls_lam
    a = jnp.exp(log_a)
    b = jnp.sqrt(1.0 - a * a) * (i * xc)
    return a, b


def _lru_prompt_kernel(x_ref, g_ref, cw_ref, cb_ref, w_ref, ba_ref, bx_ref, lam_ref,
                       y_ref, hl_ref, h_scr, win_scr, *, n_meta, n_chunks):
    TC = LRU_CHUNK
    cw = cw_ref[...]
    cb = cb_ref[...]
    ba, bx = ba_ref[...], bx_ref[...]
    ls_lam = -_softplus(-lam_ref[...])
    bw = x_ref.shape[2]

    def scan_chunk(xc, n, r0, h_prev):
        a, b = _lru_gates(xc + cb, w_ref, ba, bx, ls_lam)
        k = 1
        while k < n:
            a_s = _shift_down(a, k, 1.0)
            b_s = _shift_down(b, k, 0.0)
            b = a * b_s + b
            a = a * a_s
            k *= 2
        hrows = a * h_prev + b
        y_ref[0, pl.ds(r0, n), :] = (hrows * _silu(g_ref[0, pl.ds(r0, n), :])).astype(y_ref.dtype)
        return hrows[n - 1:n, :]

    win0 = jnp.concatenate([jnp.zeros((8, bw), F32), x_ref[0, 0:n_meta, :]], axis=0)
    h_scr[...] = scan_chunk(_conv_rows(win0, cw, n_meta), n_meta, 0, jnp.zeros((1, bw), F32))

    def body(c, carry):
        r0 = pl.multiple_of(n_meta + c * TC, 16)
        h_scr[...] = scan_chunk(_conv_at(x_ref, cw, r0, TC, slice(None), win_scr), TC, r0, h_scr[...])
        return carry

    lax.fori_loop(0, n_chunks, body, 0)
    hl_ref[0] = h_scr[...]


def _lru_prompt(P, conv_w, conv_b, w_ax_bf, b_a, b_x, lam, n_meta, w_c):
    B, L, _ = P.shape
    hc, bw, _ = w_ax_bf.shape
    n_chunks = (L - n_meta) // LRU_CHUNK
    kw = conv_w.shape[0]
    vec = lambda a: a.reshape(1, w_c)
    vspec = pl.BlockSpec((1, bw), lambda b, h: (0, h))
    return pl.pallas_call(
        functools.partial(_lru_prompt_kernel, n_meta=n_meta, n_chunks=n_chunks),
        grid=(B, hc),
        in_specs=[pl.BlockSpec((1, L, bw), lambda b, h: (b, 0, h)),
                  pl.BlockSpec((1, L, bw), lambda b, h: (b, 0, hc + h)),
                  pl.BlockSpec((kw, bw), lambda b, h: (0, h)),
                  vspec,
                  pl.BlockSpec((1, bw, 2 * bw), lambda b, h: (h, 0, 0)),
                  vspec, vspec, vspec],
        out_specs=[pl.BlockSpec((1, L, bw), lambda b, h: (b, 0, h)),
                   pl.BlockSpec((1, 1, bw), lambda b, h: (b, 0, h))],
        out_shape=[jax.ShapeDtypeStruct((B, L, w_c), BF16),
                   jax.ShapeDtypeStruct((B, 1, w_c), F32)],
        scratch_shapes=[pltpu.VMEM((1, bw), F32), pltpu.VMEM((LRU_CHUNK + 8, bw), F32)],
        compiler_params=_cparams("parallel", "arbitrary"),
        name="lru_prompt",
    )(P, P, conv_w, vec(conv_b), w_ax_bf, vec(b_a), vec(b_x), vec(lam))


def _gdn_sample_kernel(alog_ref, dtb_ref, u_ref, buf_ref, cw_ref, gates_ref, gnw_ref, z_ref, s_ref,
                       o_ref, so_ref, *, H, dk, dv):
    kw = cw_ref.shape[0]
    xc = u_ref[0] * cw_ref[kw - 1:kw, :]
    for t in range(kw - 1):
        xc = xc + buf_ref[0, t:t + 1, :] * cw_ref[t:t + 1, :]
    xc = _silu(xc)
    gates = gates_ref[0]
    gnw = gnw_ref[...]
    row8 = lax.broadcasted_iota(jnp.int32, (8, dk), 0)
    di = lax.broadcasted_iota(jnp.int32, (dk, dk), 0)
    dj = lax.broadcasted_iota(jnp.int32, (dk, dk), 1)
    for h in range(H):
        q = xc[:, h * dk:(h + 1) * dk]
        k = xc[:, (H + h) * dk:(H + h + 1) * dk]
        v = xc[:, 2 * H * dk + h * dv:2 * H * dk + (h + 1) * dv]
        q = q * (lax.rsqrt(jnp.sum(q * q, axis=-1, keepdims=True) + EPS) * (dk ** -0.5))
        k = k * lax.rsqrt(jnp.sum(k * k, axis=-1, keepdims=True) + EPS)
        beta = _sigmoid(gates[:, h:h + 1])
        g = -jnp.exp(jnp.full((1, 1), alog_ref[h], F32)) * _softplus(gates[:, H + h:H + h + 1] + dtb_ref[h])
        eg = jnp.exp(g)
        S = s_ref[0, h]
        kq = jnp.where(row8 == 0, jnp.broadcast_to(k, (8, dk)),
                       jnp.where(row8 == 1, jnp.broadcast_to(q * eg, (8, dk)), 0.0))
        R = _dot(kq.astype(BF16), S.astype(BF16))
        v_new = beta * (v - eg * R[0:1])
        o = R[1:2] + jnp.sum(q * k, axis=-1, keepdims=True) * v_new
        k_col = jnp.sum(jnp.where(di == dj, jnp.broadcast_to(k, (dk, dk)), 0.0),
                        axis=1, keepdims=True)
        so_ref[0, h] = S * eg + k_col * v_new
        on = o * lax.rsqrt(jnp.mean(o * o, axis=-1, keepdims=True) + EPS) * gnw
        o_ref[0, :, h * dv:(h + 1) * dv] = (on * _silu(z_ref[0, :, h * dv:(h + 1) * dv])).astype(o_ref.dtype)


def _gdn_sample(u, buf, conv_w, gates, a_log, dt_bias, gn_w, z_a, S, H, dk, dv):
    Bd, C = u.shape
    kw = conv_w.shape[0]
    smem = pl.BlockSpec(memory_space=pltpu.SMEM)
    row = lambda n: pl.BlockSpec((1, 1, n), lambda b: (b, 0, 0))
    o, s_new = pl.pallas_call(
        functools.partial(_gdn_sample_kernel, H=H, dk=dk, dv=dv),
        grid=(Bd,),
        in_specs=[smem, smem, row(C),
                  pl.BlockSpec((1, kw - 1, C), lambda b: (b, 0, 0)),
                  pl.BlockSpec((kw, C), lambda b: (0, 0)),
                  row(2 * H),
                  pl.BlockSpec((1, dv), lambda b: (0, 0)),
                  row(H * dv),
                  pl.BlockSpec((1, H, dk, dv), lambda b: (b, 0, 0, 0))],
        out_specs=[row(H * dv), pl.BlockSpec((1, H, dk, dv), lambda b: (b, 0, 0, 0))],
        out_shape=[jax.ShapeDtypeStruct((Bd, 1, H * dv), BF16),
                   jax.ShapeDtypeStruct((Bd, H, dk, dv), F32)],
        compiler_params=_cparams("parallel"),
        name="gdn_sample",
    )(a_log, dt_bias, u.reshape(Bd, 1, C), buf, conv_w, gates.reshape(Bd, 1, 2 * H),
      gn_w.reshape(1, dv), z_a.reshape(Bd, 1, H * dv), S)
    return o.reshape(Bd, H * dv), s_new


def _sb_sample_kernel(pt_ref, q_ref, bias_ref, z_ref, *refs, H, dh, page, pps, n_steps):
    k_refs = refs[:pps]
    v_refs = refs[pps:2 * pps]
    o_ref, acc_scr, c_scr = refs[2 * pps:]
    p = pl.program_id(1)
    LW = 128
    nv = (page * H) // LW

    @pl.when(p == 0)
    def _():
        acc_scr[...] = jnp.zeros(acc_scr.shape, F32)
        c_scr[...] = jnp.zeros(c_scr.shape, F32)

    q_bf = (q_ref[0] * (dh ** -0.5)).astype(BF16)
    bias = bias_ref[...]
    lane = lax.broadcasted_iota(jnp.int32, (H, LW), 1)
    sub = lax.broadcasted_iota(jnp.int32, (H, LW), 0)
    valid = (lane % H) == sub
    ui = lax.broadcasted_iota(jnp.int32, (LW, LW), 0)
    uj = lax.broadcasted_iota(jnp.int32, (LW, LW), 1)
    U = (ui > uj).astype(BF16)

    ls_all, l1m_all = [], []
    for s in range(pps):
        kp = k_refs[s][0, 0].astype(BF16)
        z = _dot_nt(q_bf, kp) + bias
        sp = jnp.maximum(z, 0.0) + jnp.log(1.0 + jnp.exp(-jnp.abs(z)))
        ls_all.append(z - sp)
        l1m_all += [jnp.where(valid, -sp[:, i * LW:(i + 1) * LW], 0.0) for i in range(nv)]
    L = jnp.concatenate(l1m_all, axis=0)
    hi = L.astype(BF16)
    lo = (L - hi.astype(F32)).astype(BF16)
    n_l = pps * nv * H
    E2 = _dot(jnp.concatenate([hi, lo], axis=0), U)
    E = E2[:n_l] + E2[n_l:]
    tot = E[:, 0:1] + L[:, 0:1]
    acc = acc_scr[...]
    c = c_scr[...]
    for s in range(pps):
        a_parts = [None] * nv
        for i in range(nv - 1, -1, -1):
            g = s * nv + i
            loga = ls_all[s][:, i * LW:(i + 1) * LW] + E[g * H:(g + 1) * H] + c
            a_parts[i] = jnp.where(valid, jnp.exp(loga), 0.0)
            c = c + tot[g * H:(g + 1) * H]
        a = jnp.concatenate(a_parts, axis=1).astype(BF16)
        acc = acc + _dot(a, v_refs[s][0, 0].astype(BF16))
    acc_scr[...] = acc
    c_scr[...] = c

    @pl.when(p == n_steps - 1)
    def _():
        o_ref[0] = (acc * _silu(z_ref[0])).astype(o_ref.dtype)


def _sb_sample(q, z_b, sb_bias, cache_k, cache_v, page_table, layer):
    Bd, H, dh = q.shape
    _, n_phys, page, _, _ = cache_k.shape
    n_pages = page_table.shape[1]
    pps = SB_PAGES_PER_STEP if n_pages % SB_PAGES_PER_STEP == 0 else 1
    n_steps = n_pages // pps
    ck = cache_k.reshape(cache_k.shape[0], n_phys, page * H, dh)
    cv = cache_v.reshape(cache_v.shape[0], n_phys, page * H, dh)

    def page_spec(s):
        def imap(b, p, pt):
            return (layer, pt[b * n_pages + (n_pages - 1 - (p * pps + s))], 0, 0)
        return pl.BlockSpec((1, 1, page * H, dh), imap)

    hd = pl.BlockSpec((1, H, dh), lambda b, p, pt: (b, 0, 0))
    grid_spec = pltpu.PrefetchScalarGridSpec(
        num_scalar_prefetch=1,
        grid=(Bd, n_steps),
        in_specs=[hd, pl.BlockSpec((H, 1), lambda b, p, pt: (0, 0)), hd]
                 + [page_spec(s) for s in range(pps)] * 2,
        out_specs=hd,
        scratch_shapes=[pltpu.VMEM((H, dh), F32), pltpu.VMEM((H, 1), F32)],
    )
    return pl.pallas_call(
        functools.partial(_sb_sample_kernel, H=H, dh=dh, page=page, pps=pps, n_steps=n_steps),
        grid_spec=grid_spec,
        out_shape=jax.ShapeDtypeStruct((Bd, H, dh), BF16),
        compiler_params=_cparams("parallel", "arbitrary"),
        name="sb_sample",
    )(page_table.reshape(-1), q, sb_bias.reshape(H, 1), z_b, *([ck] * pps), *([cv] * pps))


def _lru_sample_kernel(x_ref, g_ref, buf_ref, cw_ref, cb_ref, w_ref, ba_ref, bx_ref, lam_ref, h0_ref,
                       y_ref, h_ref):
    kw = cw_ref.shape[0]
    xc = x_ref[...] * cw_ref[kw - 1:kw, :] + cb_ref[...]
    for t in range(kw - 1):
        xc = xc + buf_ref[t] * cw_ref[t:t + 1, :]
    ls_lam = -_softplus(-lam_ref[...])
    a, b = _lru_gates(xc, w_ref, ba_ref[...], bx_ref[...], ls_lam)
    hn = a * h0_ref[...] + b
    h_ref[...] = hn
    y_ref[...] = (hn * _silu(g_ref[...])).astype(y_ref.dtype)


def _lru_sample(P, buf_t, conv_w, conv_b, w_ax_bf, b_a, b_x, lam, h0, w_c):
    Bd = P.shape[0]
    hc, bw, _ = w_ax_bf.shape
    kw = conv_w.shape[0]
    vec = lambda a: a.reshape(1, w_c)
    vspec = pl.BlockSpec((1, bw), lambda h: (0, h))
    mspec = lambda off: pl.BlockSpec((Bd, bw), lambda h: (0, off + h))
    return pl.pallas_call(
        _lru_sample_kernel,
        grid=(hc,),
        in_specs=[mspec(0), mspec(hc),
                  pl.BlockSpec((kw - 1, Bd, bw), lambda h: (0, 0, h)),
                  pl.BlockSpec((kw, bw), lambda h: (0, h)),
                  vspec,
                  pl.BlockSpec((1, bw, 2 * bw), lambda h: (h, 0, 0)),
                  vspec, vspec, vspec, mspec(0)],
        out_specs=[mspec(0), mspec(0)],
        out_shape=[jax.ShapeDtypeStruct((Bd, w_c), BF16), jax.ShapeDtypeStruct((Bd, w_c), F32)],
        compiler_params=_cparams("parallel"),
        name="lru_sample",
    )(P, P, buf_t, conv_w, vec(conv_b), w_ax_bf, vec(b_a), vec(b_x), vec(lam), h0)


def kernel(x_prompt, x_sample, cache_sb_k, cache_sb_v, state_gdn, state_gdn_conv, state_lru,
           state_lru_conv, page_table, meta_tokens, norm_w, final_norm_w, w_in_even, gdn_conv_w,
           gdn_a_log, gdn_dt_bias, gdn_norm_w, sb_bias, w_out_even, w_in_odd, lru_conv_w, lru_conv_b,
           lru_w_a, lru_b_a, lru_w_x, lru_b_x, lru_lambda, w_out_odd):
    B, seq, D = x_prompt.shape
    Bd = x_sample.shape[0]
    assert x_sample.shape[1] == 1
    n_meta = meta_tokens.shape[0]
    L = n_meta + seq
    depth = norm_w.shape[0]
    _, _, H_a, dk, dv = state_gdn.shape
    _, _, _, H_b, dh = cache_sb_k.shape
    w_a_tot, w_b_tot = H_a * dv, H_b * dh
    qkv_a = 2 * H_a * dk + H_a * dv
    off_qkv_b = qkv_a
    off_z_a = off_qkv_b + 3 * w_b_tot
    off_z_b = off_z_a + w_a_tot
    off_gate = off_z_b + w_b_tot
    w_c = state_lru.shape[2]
    assert dk == dv == dh == 128 and n_meta % 16 == 0 and n_meta <= GDN_CHUNK
    assert seq % LRU_CHUNK == 0 and seq % SB_BLOCK == 0 and seq % GDN_CHUNK == 0
    assert w_in_even.shape[2] == off_gate + 2 * H_a and lru_w_a.shape[2] == 128

    meta = jnp.broadcast_to(meta_tokens.astype(x_prompt.dtype)[None], (B, n_meta, D))
    xp = jnp.concatenate([meta, x_prompt], axis=1).reshape(B * L, D)
    xs = x_sample.reshape(Bd, D)
    xn_p, xn_s = _norm(xp, norm_w[0]), _norm(xs, norm_w[0])

    n_chunks = seq // GDN_CHUNK
    pk, pv, pS, pcg, ph, pcl = [], [], [], [], [], []
    sk, sv, sS, scg, sh, scl = [], [], [], [], [], []
    for layer in range(depth):
        j = layer // 2
        nw_next = norm_w[layer + 1] if layer + 1 < depth else final_norm_w
        if layer % 2 == 0:
            w_bf = w_in_even[j].astype(BF16)
            w_gate_bf = w_bf[:, off_gate:]
            w_out_bf = w_out_even[j].astype(BF16)
            P, gates = _inproj(xn_p, w_bf, off_gate, w_gate_bf)
            P3 = P.reshape(B, L, off_gate)
            g3 = jnp.swapaxes(gates.reshape(B, L, 2 * H_a), 1, 2)
            g_meta = jnp.pad(g3[:, :, :n_meta], ((0, 0), (0, 0), (0, GDN_CHUNK - n_meta)))
            gates_c = jnp.concatenate(
                [g_meta[:, :, None, :], g3[:, :, n_meta:].reshape(B, 2 * H_a, n_chunks, GDN_CHUNK)], axis=2)
            la, S_p = _gdn_prompt(P3, gates_c, gdn_conv_w[j], gdn_a_log[j], gdn_dt_bias[j], gdn_norm_w[j],
                                  n_meta, H_a, dk, dv, off_z_a)
            lb, k_p, v_p = _sb_prompt(P3, sb_bias[j], n_meta, H_b, dh, off_qkv_b, off_z_b)
            xp, xn_p = _outproj([la.reshape(B * L, w_a_tot), lb.reshape(B * L, w_b_tot)], w_out_bf, xp, nw_next)
            pk.append(k_p.reshape(B, L, H_b, dh)); pv.append(v_p.reshape(B, L, H_b, dh))
            pS.append(S_p); pcg.append(P3[:, L - (gdn_conv_w.shape[1] - 1):, :qkv_a])
            Ps, gs = _inproj(xn_s, w_bf, off_gate, w_gate_bf)
            u = Ps[:, :qkv_a]
            las, S_s = _gdn_sample(u, state_gdn_conv[j], gdn_conv_w[j], gs, gdn_a_log[j], gdn_dt_bias[j],
                                   gdn_norm_w[j], Ps[:, off_z_a:off_z_a + w_a_tot], state_gdn[j], H_a, dk, dv)
            q_s = Ps[:, off_qkv_b:off_qkv_b + w_b_tot].reshape(Bd, H_b, dh)
            k_s = Ps[:, off_qkv_b + w_b_tot:off_qkv_b + 2 * w_b_tot]
            v_s = Ps[:, off_qkv_b + 2 * w_b_tot:off_qkv_b + 3 * w_b_tot]
            z_s = Ps[:, off_z_b:off_z_b + w_b_tot].reshape(Bd, H_b, dh)
            lbs = _sb_sample(q_s, z_s, sb_bias[j], cache_sb_k, cache_sb_v, page_table, j)
            xs, xn_s = _outproj([las, lbs.reshape(Bd, w_b_tot)], w_out_bf, xs, nw_next)
            sk.append(k_s.reshape(Bd, 1, H_b, dh)); sv.append(v_s.reshape(Bd, 1, H_b, dh))
            sS.append(S_s)
            scg.append(jnp.concatenate([state_gdn_conv[j][:, 1:], u[:, None, :]], axis=1))
        else:
            w_bf = w_in_odd[j].astype(BF16)
            w_out_bf = w_out_odd[j].astype(BF16)
            w_ax_bf = jnp.concatenate([lru_w_a[j], lru_w_x[j]], axis=-1).astype(BF16)
            lru_args = (lru_conv_w[j], lru_conv_b[j], w_ax_bf, lru_b_a[j], lru_b_x[j], lru_lambda[j])
            P = _inproj(xn_p, w_bf, 2 * w_c)
            P3 = P.reshape(B, L, 2 * w_c)
            y, h_last = _lru_prompt(P3, *lru_args, n_meta, w_c)
            xp, xn_p = _outproj([y.reshape(B * L, w_c)], w_out_bf, xp, nw_next)
            ph.append(h_last.reshape(B, w_c)); pcl.append(P3[:, L - (lru_conv_w.shape[1] - 1):, :w_c])
            Ps = _inproj(xn_s, w_bf, 2 * w_c)
            ys, h_s = _lru_sample(Ps, jnp.swapaxes(state_lru_conv[j], 0, 1), *lru_args, state_lru[j], w_c)
            xs, xn_s = _outproj([ys], w_out_bf, xs, nw_next)
            sh.append(h_s)
            scl.append(jnp.concatenate([state_lru_conv[j][:, 1:], Ps[:, None, :w_c]], axis=1))

    y_prompt = _final_norm(xp.reshape(B, L, D), final_norm_w, n_meta)
    y_sample = _final_norm(xs.reshape(1, Bd, D), final_norm_w, 0).reshape(Bd, 1, D)
    st = jnp.stack
    return (y_prompt, y_sample, st(pk), st(pv), st(pS), st(pcg), st(ph), st(pcl),
            st(sk), st(sv), st(sS), st(scg), st(sh), st(scl))
```

```python
import functools
import math

import jax
import jax.numpy as jnp
from jax import lax
from jax.experimental import pallas as pl
from jax.experimental.pallas import tpu as pltpu

EPS = 1e-6
RG_C = 8.0
GDN_CHUNK = 64
SB_BLOCK = 128
SB_GROUP = 256
LRU_CHUNK = 256
LRU_CHUNKS_PER_ITER = 2
GDN_HEADS_PER_STEP = 4
GDN_CHUNKS_PER_ITER = 4
SB_PAGES_PER_STEP = 8
F32 = jnp.float32
BF16 = jnp.bfloat16
V7X_VMEM_LIMIT_BYTES = 56 * 1024 * 1024
NEG_BIG = -1e30


def _cparams(*dims):
    return pltpu.CompilerParams(dimension_semantics=dims, vmem_limit_bytes=V7X_VMEM_LIMIT_BYTES)


def _row_tile(n, cap, mult=16):
    best = None
    for t in range(mult, min(n, cap) + 1, mult):
        if n % t == 0:
            best = t
    return best if best is not None else n


def _sigmoid(x):
    return 1.0 / (1.0 + jnp.exp(-x))


def _silu(x):
    return x * _sigmoid(x)


def _softplus(x):
    return jnp.maximum(x, 0.0) + jnp.log1p(jnp.exp(-jnp.abs(x)))


def _dot(a, b):
    return jnp.dot(a, b, preferred_element_type=F32)


def _dot_nt(a, b):
    return lax.dot_general(a, b, (((1,), (1,)), ((), ())), preferred_element_type=F32)


def _dot_tn(a, b):
    return lax.dot_general(a, b, (((0,), (0,)), ((), ())), preferred_element_type=F32)


def _shift_down(x, k, fill):
    if k % 8 == 0:
        return jnp.concatenate([jnp.full((k, x.shape[1]), fill, x.dtype), x[:x.shape[0] - k]], axis=0)
    rolled = pltpu.roll(x, k, 0)
    rows = lax.broadcasted_iota(jnp.int32, x.shape, 0)
    return jnp.where(rows >= k, rolled, fill)


def _rmsnorm_bf16(xf, nw):
    ms = jnp.mean(xf * xf, axis=-1, keepdims=True)
    return (xf * lax.rsqrt(ms + EPS) * nw).astype(BF16)


def _norm_kernel(x_ref, nw_ref, o_ref):
    o_ref[...] = _rmsnorm_bf16(x_ref[...], nw_ref[...])


def _norm(x, nw, tm_cap=128):
    T, D = x.shape
    tm = _row_tile(T, tm_cap)
    return pl.pallas_call(
        _norm_kernel,
        grid=(T // tm,),
        in_specs=[pl.BlockSpec((tm, D), lambda i: (i, 0)), pl.BlockSpec((1, D), lambda i: (0, 0))],
        out_specs=pl.BlockSpec((tm, D), lambda i: (i, 0)),
        out_shape=jax.ShapeDtypeStruct((T, D), BF16),
        compiler_params=_cparams("parallel"),
        name="norm",
    )(x, nw.reshape(1, D))


def _embed_kernel(x_ref, meta_ref, nw_ref, xp_ref, xn_ref, *, n_meta, rc):
    nw = nw_ref[...]
    meta = meta_ref[...]
    xp_ref[0, 0:n_meta, :] = meta
    xn_ref[0, 0:n_meta, :] = _rmsnorm_bf16(meta, nw)

    def body(c, carry):
        r = pl.multiple_of(c * rc, rc)
        ro = pl.multiple_of(n_meta + c * rc, 16)
        xf = x_ref[0, pl.ds(r, rc), :]
        xp_ref[0, pl.ds(ro, rc), :] = xf
        xn_ref[0, pl.ds(ro, rc), :] = _rmsnorm_bf16(xf, nw)
        return carry

    lax.fori_loop(0, x_ref.shape[1] // rc, body, 0)


def _embed(x, meta, nw):
    B, S, D = x.shape
    n_meta = meta.shape[0]
    L = n_meta + S
    rc = _row_tile(S, 128)
    full = pl.BlockSpec((1, L, D), lambda b: (b, 0, 0))
    return pl.pallas_call(
        functools.partial(_embed_kernel, n_meta=n_meta, rc=rc),
        grid=(B,),
        in_specs=[pl.BlockSpec((1, S, D), lambda b: (b, 0, 0)),
                  pl.BlockSpec((n_meta, D), lambda b: (0, 0)),
                  pl.BlockSpec((1, D), lambda b: (0, 0))],
        out_specs=[full, full],
        out_shape=[jax.ShapeDtypeStruct((B, L, D), F32), jax.ShapeDtypeStruct((B, L, D), BF16)],
        compiler_params=_cparams("parallel"),
        name="embed",
    )(x, meta, nw.reshape(1, D))


def _inproj_kernel(*refs, has_small):
    if has_small:
        x_ref, w_ref, ws_ref, o_ref, os_ref = refs

        @pl.when(pl.program_id(1) == 0)
        def _():
            os_ref[...] = _dot(x_ref[...], ws_ref[...])
    else:
        x_ref, w_ref, o_ref = refs
    o_ref[...] = _dot(x_ref[...], w_ref[...])


def _inproj(xn, w_bf, n_main, w_small_bf=None, tn=512, tm_cap=2064):
    T, D = xn.shape
    tm = _row_tile(T, tm_cap)
    has_small = w_small_bf is not None
    in_specs = [pl.BlockSpec((tm, D), lambda i, j: (i, 0)),
                pl.BlockSpec((D, tn), lambda i, j: (0, j))]
    out_shape = [jax.ShapeDtypeStruct((T, n_main), F32)]
    out_specs = [pl.BlockSpec((tm, tn), lambda i, j: (i, j))]
    args = [xn, w_bf]
    if has_small:
        ns = w_small_bf.shape[1]
        in_specs.append(pl.BlockSpec((D, ns), lambda i, j: (0, 0)))
        out_shape.append(jax.ShapeDtypeStruct((T, ns), F32))
        out_specs.append(pl.BlockSpec((tm, ns), lambda i, j: (i, 0)))
        args.append(w_small_bf)
    out = pl.pallas_call(
        functools.partial(_inproj_kernel, has_small=has_small),
        grid=(T // tm, n_main // tn),
        in_specs=in_specs, out_specs=out_specs, out_shape=out_shape,
        compiler_params=_cparams("parallel", "arbitrary"),
        name="inproj",
    )(*args)
    return out if has_small else out[0]


def _outproj_kernel(*refs, n_parts):
    lhs_refs = refs[:n_parts]
    w_ref, x_ref, nw_ref, o_ref, on_ref = refs[n_parts:]
    acc = x_ref[...]
    k0 = 0
    for lr in lhs_refs:
        kk = lr.shape[1]
        acc = acc + _dot(lr[...], w_ref[k0:k0 + kk, :])
        k0 += kk
    o_ref[...] = acc
    on_ref[...] = _rmsnorm_bf16(acc, nw_ref[...])


def _outproj(lhs_parts, w_bf, x, nw_next, tm_cap=384):
    T, D = x.shape
    tm = _row_tile(T, tm_cap)
    row = pl.BlockSpec((tm, D), lambda i: (i, 0))
    in_specs = [pl.BlockSpec((tm, p.shape[1]), lambda i: (i, 0)) for p in lhs_parts]
    in_specs += [pl.BlockSpec(w_bf.shape, lambda i: (0, 0)), row, pl.BlockSpec((1, D), lambda i: (0, 0))]
    return pl.pallas_call(
        functools.partial(_outproj_kernel, n_parts=len(lhs_parts)),
        grid=(T // tm,),
        in_specs=in_specs,
        out_specs=[row, row],
        out_shape=[jax.ShapeDtypeStruct((T, D), F32), jax.ShapeDtypeStruct((T, D), BF16)],
        compiler_params=_cparams("parallel"),
        name="outproj",
    )(*lhs_parts, w_bf, x, nw_next.reshape(1, D))


def _final_norm_kernel(x_ref, w_ref, o_ref, *, n_skip, tr):
    t = pl.program_id(1)
    r = pl.multiple_of(n_skip + t * tr, 8)
    xf = x_ref[0, pl.ds(r, tr), :]
    ms = jnp.mean(xf * xf, axis=-1, keepdims=True)
    o_ref[0] = xf * lax.rsqrt(ms + EPS) * w_ref[...]


def _final_norm(x, w, n_skip):
    B, L, D = x.shape
    Lo = L - n_skip
    tr = _row_tile(Lo, 512, 8)
    return pl.pallas_call(
        functools.partial(_final_norm_kernel, n_skip=n_skip, tr=tr),
        grid=(B, Lo // tr),
        in_specs=[pl.BlockSpec((1, L, D), lambda b, t: (b, 0, 0)),
                  pl.BlockSpec((1, D), lambda b, t: (0, 0))],
        out_specs=pl.BlockSpec((1, tr, D), lambda b, t: (b, t, 0)),
        out_shape=jax.ShapeDtypeStruct((B, Lo, D), F32),
        compiler_params=_cparams("parallel", "arbitrary"),
        name="final_norm",
    )(x, w.reshape(1, D))


def _conv_rows(win, cw, n):
    kw = cw.shape[0]
    out = win[8:8 + n] * cw[kw - 1:kw, :]
    for tau in range(1, kw):
        out = out + pltpu.roll(win, tau, 0)[8:8 + n] * cw[kw - 1 - tau:kw - tau, :]
    return out


def _conv_at(ref, cw, r0, n, lanes, scr):
    kw = cw.shape[0]
    scr[...] = ref[0, pl.ds(pl.multiple_of(r0 - 8, 8), n + 8), lanes]
    out = scr[8:8 + n, :] * cw[kw - 1:kw, :]
    for tau in range(1, kw):
        out = out + scr[8 - tau:8 - tau + n, :] * cw[kw - 1 - tau:kw - tau, :]
    return out


def _gdn_prompt_kernel(alog_ref, dtb_ref, q_ref, k_ref, v_ref, cwq_ref, cwk_ref, cwv_ref,
                       braw_ref, araw_ref, gnw_ref, z_ref, o_ref, s_ref, S_scr, win_scr,
                       *, n_meta, n_chunks, dk, hg):
    h0 = pl.program_id(1) * hg
    C = GDN_CHUNK
    a_coefs = [-jnp.exp(jnp.full((1, 1), alog_ref[h0 + hh], F32)) for hh in range(hg)]
    dt_bs = [dtb_ref[h0 + hh] for hh in range(hg)]
    ii = lax.broadcasted_iota(jnp.int32, (C, C), 0)
    jj = lax.broadcasted_iota(jnp.int32, (C, C), 1)
    eye = (ii == jj).astype(F32)
    gnw = gnw_ref[...]

    def conv(ref, cw_ref, r0, first, lanes, slot):
        if first:
            win = jnp.concatenate([jnp.zeros((8, dk), F32), ref[0, 0:C, lanes]], axis=0)
            return _conv_rows(win, cw_ref[:, lanes], C)
        return _conv_at(ref, cw_ref[:, lanes], r0, C, lanes, win_scr.at[slot])

    def solve_stage(probs):
        qs, ks, vs, gates = [], [], [], []
        for p, (hh, r0, crow, first) in enumerate(probs):
            lanes = slice(hh * dk, (hh + 1) * dk)
            q = _silu(conv(q_ref, cwq_ref, r0, first, lanes, 3 * p))
            k = _silu(conv(k_ref, cwk_ref, r0, first, lanes, 3 * p + 1))
            v = _silu(conv(v_ref, cwv_ref, r0, first, lanes, 3 * p + 2))
            qs.append(q * (lax.rsqrt(jnp.sum(q * q, axis=-1, keepdims=True) + EPS) * (dk ** -0.5)))
            ks.append(k * lax.rsqrt(jnp.sum(k * k, axis=-1, keepdims=True) + EPS))
            vs.append(v)
            beta_row = _sigmoid(braw_ref[0, hh, pl.ds(crow, 1), :])
            g_row = a_coefs[hh] * _softplus(araw_ref[0, hh, pl.ds(crow, 1), :] + dt_bs[hh])
            if first:
                valid = lax.broadcasted_iota(jnp.int32, (1, C), 1) < n_meta
                beta_row = jnp.where(valid, beta_row, 0.0)
                g_row = jnp.where(valid, g_row, 0.0)
            g_rb = jnp.broadcast_to(g_row, (C, C))
            gc_col = jnp.sum(jnp.where(jj <= ii, g_rb, 0.0), axis=1, keepdims=True)
            g_col = jnp.sum(jnp.where(jj == ii, g_rb, 0.0), axis=1, keepdims=True)
            gc_row = jnp.sum(jnp.where(ii <= jj, jnp.broadcast_to(g_col, (C, C)), 0.0),
                             axis=0, keepdims=True)
            beta_col = jnp.sum(jnp.where(jj == ii, jnp.broadcast_to(beta_row, (C, C)), 0.0),
                               axis=1, keepdims=True)
            g_last = jnp.sum(g_row, axis=1, keepdims=True)
            decay = jnp.exp(jnp.where(ii >= jj, gc_col - gc_row, NEG_BIG))
            gates.append((beta_col, gc_col, g_last, decay, jnp.exp(gc_col)))
        n = len(probs)
        kbs = [ks[p] * gates[p][0] for p in range(n)]
        Xs = [_dot_nt(jnp.concatenate([qs[p], kbs[p]], axis=0).astype(BF16), ks[p].astype(BF16))
              for p in range(n)]
        qks = [Xs[p][:C] * gates[p][3] for p in range(n)]
        Ns = [-jnp.where(ii > jj, Xs[p][C:] * gates[p][3], 0.0) for p in range(n)]
        Tinvs = [eye + N for N in Ns]
        for _ in range(max(1, int(math.ceil(math.log2(C))) - 1)):
            Ns = [_dot(N.astype(BF16), N.astype(BF16)) for N in Ns]
            Tinvs = [Tinvs[p] + _dot(Tinvs[p].astype(BF16), Ns[p].astype(BF16)) for p in range(n)]
        sols = [_dot(Tinvs[p].astype(BF16),
                     jnp.concatenate([vs[p] * gates[p][0], kbs[p] * gates[p][4]], axis=1).astype(BF16))
                for p in range(n)]
        dv = vs[0].shape[1]
        return [(sols[p][:, :dv], sols[p][:, dv:], qks[p], qs[p] * gates[p][4],
                 ks[p] * jnp.exp(gates[p][2] - gates[p][1]), jnp.exp(gates[p][2])) for p in range(n)]

    def state_stage(items, r0, n_store):
        Ss = [S_scr[hh] for hh in range(hg)]
        WQs = [_dot(jnp.concatenate([items[hh][1], items[hh][3]], axis=0).astype(BF16), Ss[hh].astype(BF16))
               for hh in range(hg)]
        v_news = [items[hh][0] - WQs[hh][:C] for hh in range(hg)]
        os_ = [WQs[hh][C:] + _dot(items[hh][2].astype(BF16), v_news[hh].astype(BF16)) for hh in range(hg)]
        S_news = [Ss[hh] * items[hh][5] + _dot_tn(items[hh][4].astype(BF16), v_news[hh].astype(BF16))
                  for hh in range(hg)]
        for hh in range(hg):
            lanes = slice(hh * dk, (hh + 1) * dk)
            S_scr[hh] = S_news[hh]
            o = os_[hh]
            on = o * lax.rsqrt(jnp.mean(o * o, axis=-1, keepdims=True) + EPS) * gnw
            o_ref[0, pl.ds(r0, n_store), lanes] = (
                on[:n_store] * _silu(z_ref[0, pl.ds(r0, n_store), lanes])).astype(o_ref.dtype)

    S_scr[...] = jnp.zeros(S_scr.shape, F32)
    state_stage(solve_stage([(hh, 0, 0, True) for hh in range(hg)]), 0, n_meta)

    cb = GDN_CHUNKS_PER_ITER if n_chunks % GDN_CHUNKS_PER_ITER == 0 else 1

    def body(t, carry):
        r0s = [pl.multiple_of(n_meta + (t * cb + cc) * C, 16) for cc in range(cb)]
        items = solve_stage([(hh, r0s[cc], t * cb + cc + 1, False) for cc in range(cb) for hh in range(hg)])
        for cc in range(cb):
            state_stage(items[cc * hg:(cc + 1) * hg], r0s[cc], C)
        return carry

    lax.fori_loop(0, n_chunks // cb, body, 0)
    s_ref[0] = S_scr[...]


def _gdn_prompt(P, gates_c, conv_w, a_log, dt_bias, gn_w, n_meta, H, dk, dv, off_z):
    B, L, _ = P.shape
    n_chunks = (L - n_meta) // GDN_CHUNK
    hg = GDN_HEADS_PER_STEP if H % GDN_HEADS_PER_STEP == 0 else 1
    ng = H // hg
    zb0 = off_z // (hg * dv)
    cw = conv_w.shape[0]
    spec_col = lambda off: pl.BlockSpec((1, L, hg * dk), lambda b, g: (b, 0, off + g))
    spec_cw = lambda off: pl.BlockSpec((cw, hg * dk), lambda b, g: (0, off + g))
    smem = pl.BlockSpec(memory_space=pltpu.SMEM)
    nc1 = gates_c.shape[2]
    return pl.pallas_call(
        functools.partial(_gdn_prompt_kernel, n_meta=n_meta, n_chunks=n_chunks, dk=dk, hg=hg),
        grid=(B, ng),
        in_specs=[smem, smem, spec_col(0), spec_col(ng), spec_col(2 * ng),
                  spec_cw(0), spec_cw(ng), spec_cw(2 * ng),
                  pl.BlockSpec((1, hg, nc1, GDN_CHUNK), lambda b, g: (b, g, 0, 0)),
                  pl.BlockSpec((1, hg, nc1, GDN_CHUNK), lambda b, g: (b, ng + g, 0, 0)),
                  pl.BlockSpec((1, dv), lambda b, g: (0, 0)),
                  pl.BlockSpec((1, L, hg * dv), lambda b, g: (b, 0, zb0 + g))],
        out_specs=[pl.BlockSpec((1, L, hg * dv), lambda b, g: (b, 0, g)),
                   pl.BlockSpec((1, hg, dk, dv), lambda b, g: (b, g, 0, 0))],
        out_shape=[jax.ShapeDtypeStruct((B, L, H * dv), BF16),
                   jax.ShapeDtypeStruct((B, H, dk, dv), F32)],
        scratch_shapes=[pltpu.VMEM((hg, dk, dv), F32),
                        pltpu.VMEM((3 * hg * GDN_CHUNKS_PER_ITER, GDN_CHUNK + 8, dk), F32)],
        compiler_params=_cparams("parallel", "arbitrary"),
        name="gdn_prompt",
    )(a_log, dt_bias, P, P, P, conv_w, conv_w, conv_w, gates_c, gates_c,
      gn_w.reshape(1, dv), P)


def _sb_prompt_kernel(*refs, n_fill, **kw):
    ins, outs = refs[:5], refs[-3:]
    h = pl.program_id(1)
    if n_fill == 1:
        _sb_prompt_body(h, *ins, *outs, **kw)
        return
    layer_step = pl.program_id(2)

    @pl.when(layer_step == 0)
    def _():
        _sb_prompt_body(h, *ins, *outs, **kw)

    @pl.when(layer_step > 0)
    def _():
        outs[1][...] = jnp.zeros(outs[1].shape, F32)
        outs[2][...] = jnp.zeros(outs[2].shape, F32)


def _sb_prompt_body(h, bias_ref, q_ref, k_ref, v_ref, z_ref, o_ref, ko_ref, vo_ref,
                    *, n_meta, n_qb, qb, dh):
    MB = SB_BLOCK
    bias = bias_ref[h]
    scale = dh ** -0.5

    def strict_lower(n):
        return jnp.where(lax.broadcasted_iota(jnp.int32, (n, n), 0)
                         > lax.broadcasted_iota(jnp.int32, (n, n), 1), -1.0, 0.0).astype(BF16)

    gw_main = min(qb, SB_GROUP)
    U_main = strict_lower(gw_main)
    U_meta = U_main if gw_main == MB else strict_lower(MB)

    ko_ref[...] = k_ref[...]
    vo_ref[...] = v_ref[...]

    def tile(q_bf, ks, kw, mode, carry):
        acc, c = carry
        rows = q_bf.shape[0]
        kblk = k_ref[0, pl.ds(ks, kw), :].astype(BF16)
        vblk = v_ref[0, pl.ds(ks, kw), :].astype(BF16)
        z = _dot_nt(q_bf, kblk) + bias
        sp = jnp.maximum(z, 0.0) + jnp.log(1.0 + jnp.exp(-jnp.abs(z)))
        ls = z - sp
        if mode is None:
            vis = None
        else:
            col = lax.broadcasted_iota(jnp.int32, (rows, kw), 1)
            vis = (col < lax.broadcasted_iota(jnp.int32, (rows, kw), 0)) if mode == "diag" else (col < n_meta)
            sp = jnp.where(vis, sp, 0.0)
        gw = min(kw, SB_GROUP)
        U = U_main if gw == gw_main else U_meta
        sp_bf = sp.astype(BF16)
        a_parts = [None] * (kw // gw)
        for g in range(kw // gw - 1, -1, -1):
            sl = slice(g * gw, (g + 1) * gw)
            E = _dot(sp_bf[:, sl], U)
            a = jnp.exp(ls[:, sl] + E + c)
            a_parts[g] = a if vis is None else jnp.where(vis[:, sl], a, 0.0)
            c = c + (E[:, 0:1] - sp[:, g * gw:g * gw + 1])
        a_all = a_parts[0] if len(a_parts) == 1 else jnp.concatenate(a_parts, axis=1)
        return acc + _dot(a_all.astype(BF16), vblk), c

    def zero(rows):
        return (jnp.zeros((rows, dh), F32), jnp.zeros((rows, 1), F32))

    acc0, _ = tile((q_ref[0, 0:MB, :] * scale).astype(BF16), 0, MB, "diag", zero(MB))
    o_ref[0, 0:n_meta, :] = (acc0[:n_meta] * _silu(z_ref[0, 0:n_meta, :])).astype(o_ref.dtype)

    def qblock(i, carry):
        qs = pl.multiple_of(n_meta + i * qb, 16)
        q_bf = (q_ref[0, pl.ds(qs, qb), :] * scale).astype(BF16)
        st = tile(q_bf, qs, qb, "diag", zero(qb))

        def kstep(t, st):
            ks = pl.multiple_of(n_meta + (i - 1 - t) * qb, 16)
            return tile(q_bf, ks, qb, None, st)

        st = lax.fori_loop(0, i, kstep, st)
        acc, _ = tile(q_bf, 0, MB, "meta", st)
        o_ref[0, pl.ds(qs, qb), :] = (acc * _silu(z_ref[0, pl.ds(qs, qb), :])).astype(o_ref.dtype)
        return carry

    lax.fori_loop(0, n_qb, qblock, 0)


def _sb_prompt(P, sb_bias, n_meta, H, dh, off_q, off_z, layer, n_layers, kv_slabs=None):
    B, L, _ = P.shape
    qb = next(c for c in (512, 256, 128) if (L - n_meta) % c == 0)
    n_qb = (L - n_meta) // qb
    qb0, zb0 = off_q // dh, off_z // dh
    n_fill = n_layers if kv_slabs is None else 1
    col = lambda off: pl.BlockSpec((1, L, dh), lambda b, h, s: (b, 0, off + h))
    slab = pl.BlockSpec((None, 1, L, dh), lambda b, h, s: ((layer + s) % n_layers, b, 0, h))
    in_specs = [pl.BlockSpec(memory_space=pltpu.SMEM), col(qb0), col(qb0 + H), col(qb0 + 2 * H), col(zb0)]
    args = [sb_bias, P, P, P, P]
    aliases = {}
    if kv_slabs is not None:
        in_specs += [pl.BlockSpec(memory_space=pl.ANY)] * 2
        aliases = {len(args): 1, len(args) + 1: 2}
        args += list(kv_slabs)
    slab_shape = jax.ShapeDtypeStruct((n_layers, B, L, H * dh), F32)
    return pl.pallas_call(
        functools.partial(_sb_prompt_kernel, n_fill=n_fill, n_meta=n_meta, n_qb=n_qb, qb=qb, dh=dh),
        grid=(B, H, n_fill),
        in_specs=in_specs,
        out_specs=[col(0), slab, slab],
        out_shape=[jax.ShapeDtypeStruct((B, L, H * dh), BF16), slab_shape, slab_shape],
        input_output_aliases=aliases,
        compiler_params=_cparams("parallel", "arbitrary", "arbitrary"),
        name="sb_prompt",
    )(*args)


def _lru_gates(xc, w_ref, ba, bx, ls_lam):
    bw = xc.shape[1]
    pre = _dot(xc.astype(BF16), w_ref[0])
    r = _sigmoid(pre[:, :bw] + ba)
    i = _sigmoid(pre[:, bw:] + bx)
    log_a = RG_C * r * ls_lam
    a = jnp.exp(log_a)
    om = 1.0 - a * a
    root = jnp.where(om > 0.0, om * lax.rsqrt(om), 0.0)
    return a, root * (i * xc)


def _lru_prompt_kernel(x_ref, g_ref, cw_ref, cb_ref, w_ref, ba_ref, bx_ref, lam_ref,
                       y_ref, hl_ref, h_scr, win_scr, ab_scr, hin_scr, *, n_meta, n_chunks):
    TC = LRU_CHUNK
    cw = cw_ref[...]
    cb = cb_ref[...]
    ba, bx = ba_ref[...], bx_ref[...]
    ls_lam = -_softplus(-lam_ref[...])
    bw = x_ref.shape[2]

    def scan_chunk(xc, n, r0, h_prev):
        a, b = _lru_gates(xc + cb, w_ref, ba, bx, ls_lam)
        k = 1
        while k < n:
            a_s = _shift_down(a, k, 1.0)
            b_s = _shift_down(b, k, 0.0)
            b = a * b_s + b
            a = a * a_s
            k *= 2
        hrows = a * h_prev + b
        y_ref[0, pl.ds(r0, n), :] = (hrows * _silu(g_ref[0, pl.ds(r0, n), :])).astype(y_ref.dtype)
        return hrows[n - 1:n, :]

    def scan_prep(r0, slot):
        xc = _conv_at(x_ref, cw, r0, TC, slice(None), win_scr.at[slot])
        a, b = _lru_gates(xc + cb, w_ref, ba, bx, ls_lam)
        sub = lax.broadcasted_iota(jnp.int32, (TC, bw), 0) % 8
        for k in (1, 2, 4):
            keep = sub >= k
            a_s = jnp.where(keep, pltpu.roll(a, k, 0), 1.0)
            b_s = jnp.where(keep, pltpu.roll(b, k, 0), 0.0)
            b = a * b_s + b
            a = a * a_s
        ng = TC // 8
        ab_scr[slot, 0] = a
        ab_scr[slot, 1] = b
        at = ab_scr[slot, 0, pl.ds(7, ng, stride=8), :]
        bt = ab_scr[slot, 1, pl.ds(7, ng, stride=8), :]
        k = 1
        while k < ng:
            at_s = _shift_down(at, k, 1.0)
            bt_s = _shift_down(bt, k, 0.0)
            bt = at * bt_s + bt
            at = at * at_s
            k *= 2
        return a, b, _shift_down(at, 1, 1.0), _shift_down(bt, 1, 0.0)

    def scan_finish(prep, r0, slot, h_prev):
        a, b, atx, btx = prep
        ng = TC // 8
        hin_scr[slot] = atx * h_prev + btx
        hrows = jnp.concatenate(
            [a[8 * g:8 * g + 8] * hin_scr[slot, g:g + 1, :] + b[8 * g:8 * g + 8] for g in range(ng)], axis=0)
        y_ref[0, pl.ds(r0, TC), :] = (hrows * _silu(g_ref[0, pl.ds(r0, TC), :])).astype(y_ref.dtype)
        return hrows[TC - 1:TC, :]

    win0 = jnp.concatenate([jnp.zeros((8, bw), F32), x_ref[0, 0:n_meta, :]], axis=0)
    h_scr[...] = scan_chunk(_conv_rows(win0, cw, n_meta), n_meta, 0, jnp.zeros((1, bw), F32))

    cb_n = LRU_CHUNKS_PER_ITER if n_chunks % LRU_CHUNKS_PER_ITER == 0 else 1

    def body(t, carry):
        r0s = [pl.multiple_of(n_meta + (t * cb_n + cc) * TC, 16) for cc in range(cb_n)]
        preps = [scan_prep(r0s[cc], cc) for cc in range(cb_n)]
        h = h_scr[...]
        for cc in range(cb_n):
            h = scan_finish(preps[cc], r0s[cc], cc, h)
        h_scr[...] = h
        return carry

    lax.fori_loop(0, n_chunks // cb_n, body, 0)
    hl_ref[0] = h_scr[...]


def _lru_prompt(P, conv_w, conv_b, w_ax_bf, b_a, b_x, lam, n_meta, w_c):
    B, L, _ = P.shape
    hc, bw, _ = w_ax_bf.shape
    n_chunks = (L - n_meta) // LRU_CHUNK
    kw = conv_w.shape[0]
    vec = lambda a: a.reshape(1, w_c)
    vspec = pl.BlockSpec((1, bw), lambda b, h: (0, h))
    return pl.pallas_call(
        functools.partial(_lru_prompt_kernel, n_meta=n_meta, n_chunks=n_chunks),
        grid=(B, hc),
        in_specs=[pl.BlockSpec((1, L, bw), lambda b, h: (b, 0, h)),
                  pl.BlockSpec((1, L, bw), lambda b, h: (b, 0, hc + h)),
                  pl.BlockSpec((kw, bw), lambda b, h: (0, h)),
                  vspec,
                  pl.BlockSpec((1, bw, 2 * bw), lambda b, h: (h, 0, 0)),
                  vspec, vspec, vspec],
        out_specs=[pl.BlockSpec((1, L, bw), lambda b, h: (b, 0, h)),
                   pl.BlockSpec((1, 1, bw), lambda b, h: (b, 0, h))],
        out_shape=[jax.ShapeDtypeStruct((B, L, w_c), BF16),
                   jax.ShapeDtypeStruct((B, 1, w_c), F32)],
        scratch_shapes=[pltpu.VMEM((1, bw), F32),
                        pltpu.VMEM((LRU_CHUNKS_PER_ITER, LRU_CHUNK + 8, bw), F32),
                        pltpu.VMEM((LRU_CHUNKS_PER_ITER, 2, LRU_CHUNK, bw), F32),
                        pltpu.VMEM((LRU_CHUNKS_PER_ITER, LRU_CHUNK // 8, bw), F32)],
        compiler_params=_cparams("parallel", "arbitrary"),
        name="lru_prompt",
    )(P, P, conv_w, vec(conv_b), w_ax_bf, vec(b_a), vec(b_x), vec(lam))


def _gdn_sample_kernel(alog_ref, dtb_ref, u_ref, buf_ref, cw_ref, gates_ref, gnw_ref, z_ref, s_ref,
                       o_ref, so_ref, *, H, dk, dv):
    kw = cw_ref.shape[0]
    xc = u_ref[0] * cw_ref[kw - 1:kw, :]
    for t in range(kw - 1):
        xc = xc + buf_ref[0, t:t + 1, :] * cw_ref[t:t + 1, :]
    xc = _silu(xc)
    gates = gates_ref[0]
    gnw = gnw_ref[...]
    row8 = lax.broadcasted_iota(jnp.int32, (8, dk), 0)
    di = lax.broadcasted_iota(jnp.int32, (dk, dk), 0)
    dj = lax.broadcasted_iota(jnp.int32, (dk, dk), 1)
    for h in range(H):
        q = xc[:, h * dk:(h + 1) * dk]
        k = xc[:, (H + h) * dk:(H + h + 1) * dk]
        v = xc[:, 2 * H * dk + h * dv:2 * H * dk + (h + 1) * dv]
        q = q * (lax.rsqrt(jnp.sum(q * q, axis=-1, keepdims=True) + EPS) * (dk ** -0.5))
        k = k * lax.rsqrt(jnp.sum(k * k, axis=-1, keepdims=True) + EPS)
        beta = _sigmoid(gates[:, h:h + 1])
        g = -jnp.exp(jnp.full((1, 1), alog_ref[h], F32)) * _softplus(gates[:, H + h:H + h + 1] + dtb_ref[h])
        eg = jnp.exp(g)
        S = s_ref[0, h]
        kq = jnp.where(row8 == 0, jnp.broadcast_to(k, (8, dk)),
                       jnp.where(row8 == 1, jnp.broadcast_to(q * eg, (8, dk)), 0.0))
        R = _dot(kq.astype(BF16), S.astype(BF16))
        v_new = beta * (v - eg * R[0:1])
        o = R[1:2] + jnp.sum(q * k, axis=-1, keepdims=True) * v_new
        k_col = jnp.sum(jnp.where(di == dj, jnp.broadcast_to(k, (dk, dk)), 0.0),
                        axis=1, keepdims=True)
        so_ref[0, h] = S * eg + k_col * v_new
        on = o * lax.rsqrt(jnp.mean(o * o, axis=-1, keepdims=True) + EPS) * gnw
        o_ref[0, :, h * dv:(h + 1) * dv] = (on * _silu(z_ref[0, :, h * dv:(h + 1) * dv])).astype(o_ref.dtype)


def _gdn_sample(u, buf, conv_w, gates, a_log, dt_bias, gn_w, z_a, S, layer, H, dk, dv):
    Bd, C = u.shape
    kw = conv_w.shape[0]
    smem = pl.BlockSpec(memory_space=pltpu.SMEM)
    row = lambda n: pl.BlockSpec((1, 1, n), lambda b: (b, 0, 0))
    o, s_new = pl.pallas_call(
        functools.partial(_gdn_sample_kernel, H=H, dk=dk, dv=dv),
        grid=(Bd,),
        in_specs=[smem, smem, row(C),
                  pl.BlockSpec((1, kw - 1, C), lambda b: (b, 0, 0)),
                  pl.BlockSpec((kw, C), lambda b: (0, 0)),
                  row(2 * H),
                  pl.BlockSpec((1, dv), lambda b: (0, 0)),
                  row(H * dv),
                  pl.BlockSpec((None, 1, H, dk, dv), lambda b: (layer, b, 0, 0, 0))],
        out_specs=[row(H * dv), pl.BlockSpec((1, H, dk, dv), lambda b: (b, 0, 0, 0))],
        out_shape=[jax.ShapeDtypeStruct((Bd, 1, H * dv), BF16),
                   jax.ShapeDtypeStruct((Bd, H, dk, dv), F32)],
        compiler_params=_cparams("parallel"),
        name="gdn_sample",
    )(a_log, dt_bias, u.reshape(Bd, 1, C), buf, conv_w, gates.reshape(Bd, 1, 2 * H),
      gn_w.reshape(1, dv), z_a.reshape(Bd, 1, H * dv), S)
    return o.reshape(Bd, H * dv), s_new


def _sb_sample_kernel(pt_ref, q_ref, bias_ref, z_ref, *refs, H, dh, page, pps, n_steps):
    k_refs = refs[:pps]
    v_refs = refs[pps:2 * pps]
    o_ref, acc_scr, c_scr = refs[2 * pps:]
    p = pl.program_id(1)
    LW = 128
    nv = (page * H) // LW

    @pl.when(p == 0)
    def _():
        acc_scr[...] = jnp.zeros(acc_scr.shape, F32)
        c_scr[...] = jnp.zeros(c_scr.shape, F32)

    q_bf = (q_ref[0] * (dh ** -0.5)).astype(BF16)
    bias = bias_ref[...]
    lane = lax.broadcasted_iota(jnp.int32, (H, LW), 1)
    sub = lax.broadcasted_iota(jnp.int32, (H, LW), 0)
    valid = (lane % H) == sub
    ui = lax.broadcasted_iota(jnp.int32, (LW, LW), 0)
    uj = lax.broadcasted_iota(jnp.int32, (LW, LW), 1)
    U = (ui > uj).astype(BF16)

    ls_all, l1m_all = [], []
    for s in range(pps):
        kp = k_refs[s][0, 0].astype(BF16)
        z = _dot_nt(q_bf, kp) + bias
        sp = jnp.maximum(z, 0.0) + jnp.log(1.0 + jnp.exp(-jnp.abs(z)))
        ls_all.append(z - sp)
        l1m_all += [jnp.where(valid, -sp[:, i * LW:(i + 1) * LW], 0.0) for i in range(nv)]
    L = jnp.concatenate(l1m_all, axis=0)
    hi = L.astype(BF16)
    lo = (L - hi.astype(F32)).astype(BF16)
    n_l = pps * nv * H
    E2 = _dot(jnp.concatenate([hi, lo], axis=0), U)
    E = E2[:n_l] + E2[n_l:]
    tot = E[:, 0:1] + L[:, 0:1]
    acc = acc_scr[...]
    c = c_scr[...]
    for s in range(pps):
        a_parts = [None] * nv
        for i in range(nv - 1, -1, -1):
            g = s * nv + i
            loga = ls_all[s][:, i * LW:(i + 1) * LW] + E[g * H:(g + 1) * H] + c
            a_parts[i] = jnp.where(valid, jnp.exp(loga), 0.0)
            c = c + tot[g * H:(g + 1) * H]
        a = jnp.concatenate(a_parts, axis=1).astype(BF16)
        acc = acc + _dot(a, v_refs[s][0, 0].astype(BF16))
    acc_scr[...] = acc
    c_scr[...] = c

    @pl.when(p == n_steps - 1)
    def _():
        o_ref[0] = (acc * _silu(z_ref[0])).astype(o_ref.dtype)


def _sb_sample(q, z_b, sb_bias, cache_k, cache_v, page_table, layer):
    Bd, H, dh = q.shape
    _, n_phys, page, _, _ = cache_k.shape
    n_pages = page_table.shape[1]
    pps = SB_PAGES_PER_STEP if n_pages % SB_PAGES_PER_STEP == 0 else 1
    n_steps = n_pages // pps
    ck = cache_k.reshape(cache_k.shape[0], n_phys, page * H, dh)
    cv = cache_v.reshape(cache_v.shape[0], n_phys, page * H, dh)

    def page_spec(s):
        def imap(b, p, pt):
            return (layer, pt[b * n_pages + (n_pages - 1 - (p * pps + s))], 0, 0)
        return pl.BlockSpec((1, 1, page * H, dh), imap)

    hd = pl.BlockSpec((1, H, dh), lambda b, p, pt: (b, 0, 0))
    grid_spec = pltpu.PrefetchScalarGridSpec(
        num_scalar_prefetch=1,
        grid=(Bd, n_steps),
        in_specs=[hd, pl.BlockSpec((H, 1), lambda b, p, pt: (0, 0)), hd]
                 + [page_spec(s) for s in range(pps)] * 2,
        out_specs=hd,
        scratch_shapes=[pltpu.VMEM((H, dh), F32), pltpu.VMEM((H, 1), F32)],
    )
    return pl.pallas_call(
        functools.partial(_sb_sample_kernel, H=H, dh=dh, page=page, pps=pps, n_steps=n_steps),
        grid_spec=grid_spec,
        out_shape=jax.ShapeDtypeStruct((Bd, H, dh), BF16),
        compiler_params=_cparams("parallel", "arbitrary"),
        name="sb_sample",
    )(page_table.reshape(-1), q, sb_bias.reshape(H, 1), z_b, *([ck] * pps), *([cv] * pps))


def _lru_sample_kernel(x_ref, g_ref, buf_ref, cw_ref, cb_ref, w_ref, ba_ref, bx_ref, lam_ref, h0_ref,
                       y_ref, h_ref):
    kw = cw_ref.shape[0]
    xc = x_ref[...] * cw_ref[kw - 1:kw, :] + cb_ref[...]
    for t in range(kw - 1):
        xc = xc + buf_ref[t] * cw_ref[t:t + 1, :]
    ls_lam = -_softplus(-lam_ref[...])
    a, b = _lru_gates(xc, w_ref, ba_ref[...], bx_ref[...], ls_lam)
    hn = a * h0_ref[...] + b
    h_ref[...] = hn
    y_ref[...] = (hn * _silu(g_ref[...])).astype(y_ref.dtype)


def _lru_sample(P, buf_t, conv_w, conv_b, w_ax_bf, b_a, b_x, lam, h0, w_c):
    Bd = P.shape[0]
    hc, bw, _ = w_ax_bf.shape
    kw = conv_w.shape[0]
    vec = lambda a: a.reshape(1, w_c)
    vspec = pl.BlockSpec((1, bw), lambda h: (0, h))
    mspec = lambda off: pl.BlockSpec((Bd, bw), lambda h: (0, off + h))
    return pl.pallas_call(
        _lru_sample_kernel,
        grid=(hc,),
        in_specs=[mspec(0), mspec(hc),
                  pl.BlockSpec((kw - 1, Bd, bw), lambda h: (0, 0, h)),
                  pl.BlockSpec((kw, bw), lambda h: (0, h)),
                  vspec,
                  pl.BlockSpec((1, bw, 2 * bw), lambda h: (h, 0, 0)),
                  vspec, vspec, vspec, mspec(0)],
        out_specs=[mspec(0), mspec(0)],
        out_shape=[jax.ShapeDtypeStruct((Bd, w_c), BF16), jax.ShapeDtypeStruct((Bd, w_c), F32)],
        compiler_params=_cparams("parallel"),
        name="lru_sample",
    )(P, P, buf_t, conv_w, vec(conv_b), w_ax_bf, vec(b_a), vec(b_x), vec(lam), h0)


def kernel(x_prompt, x_sample, cache_sb_k, cache_sb_v, state_gdn, state_gdn_conv, state_lru,
           state_lru_conv, page_table, meta_tokens, norm_w, final_norm_w, w_in_even, gdn_conv_w,
           gdn_a_log, gdn_dt_bias, gdn_norm_w, sb_bias, w_out_even, w_in_odd, lru_conv_w, lru_conv_b,
           lru_w_a, lru_b_a, lru_w_x, lru_b_x, lru_lambda, w_out_odd):
    B, seq, D = x_prompt.shape
    Bd = x_sample.shape[0]
    assert x_sample.shape[1] == 1
    n_meta = meta_tokens.shape[0]
    L = n_meta + seq
    depth = norm_w.shape[0]
    _, _, H_a, dk, dv = state_gdn.shape
    _, _, _, H_b, dh = cache_sb_k.shape
    w_a_tot, w_b_tot = H_a * dv, H_b * dh
    qkv_a = 2 * H_a * dk + H_a * dv
    off_qkv_b = qkv_a
    off_z_a = off_qkv_b + 3 * w_b_tot
    off_z_b = off_z_a + w_a_tot
    off_gate = off_z_b + w_b_tot
    w_c = state_lru.shape[2]
    assert dk == dv == dh == 128 and n_meta % 16 == 0 and n_meta <= GDN_CHUNK
    assert seq % LRU_CHUNK == 0 and seq % SB_BLOCK == 0 and seq % GDN_CHUNK == 0
    assert w_in_even.shape[2] == off_gate + 2 * H_a and lru_w_a.shape[2] == 128

    xp, xn_p = _embed(x_prompt, meta_tokens.astype(x_prompt.dtype), norm_w[0])
    xp, xn_p = xp.reshape(B * L, D), xn_p.reshape(B * L, D)
    xs = x_sample.reshape(Bd, D)
    xn_s = _norm(xs, norm_w[0])

    n_chunks = seq // GDN_CHUNK
    n_even = (depth + 1) // 2
    k_slab = v_slab = None
    pS, pcg, ph, pcl = [], [], [], []
    sk, sv, sS, scg, sh, scl = [], [], [], [], [], []
    for layer in range(depth):
        j = layer // 2
        nw_next = norm_w[layer + 1] if layer + 1 < depth else final_norm_w
        if layer % 2 == 0:
            w_bf = w_in_even[j].astype(BF16)
            w_gate_bf = w_bf[:, off_gate:]
            w_out_bf = w_out_even[j].astype(BF16)
            P, gates = _inproj(xn_p, w_bf, off_gate, w_gate_bf)
            P3 = P.reshape(B, L, off_gate)
            g3 = jnp.swapaxes(gates.reshape(B, L, 2 * H_a), 1, 2)
            g_meta = jnp.pad(g3[:, :, :n_meta], ((0, 0), (0, 0), (0, GDN_CHUNK - n_meta)))
            gates_c = jnp.concatenate(
                [g_meta[:, :, None, :], g3[:, :, n_meta:].reshape(B, 2 * H_a, n_chunks, GDN_CHUNK)], axis=2)
            la, S_p = _gdn_prompt(P3, gates_c, gdn_conv_w[j], gdn_a_log[j], gdn_dt_bias[j], gdn_norm_w[j],
                                  n_meta, H_a, dk, dv, off_z_a)
            lb, k_slab, v_slab = _sb_prompt(P3, sb_bias[j], n_meta, H_b, dh, off_qkv_b, off_z_b, j, n_even,
                                            None if j == 0 else (k_slab, v_slab))
            xp, xn_p = _outproj([la.reshape(B * L, w_a_tot), lb.reshape(B * L, w_b_tot)], w_out_bf, xp, nw_next)
            pS.append(S_p); pcg.append(P3[:, L - (gdn_conv_w.shape[1] - 1):, :qkv_a])
            Ps, gs = _inproj(xn_s, w_bf, off_gate, w_gate_bf)
            u = Ps[:, :qkv_a]
            las, S_s = _gdn_sample(u, state_gdn_conv[j], gdn_conv_w[j], gs, gdn_a_log[j], gdn_dt_bias[j],
                                   gdn_norm_w[j], Ps[:, off_z_a:off_z_a + w_a_tot], state_gdn, j, H_a, dk, dv)
            q_s = Ps[:, off_qkv_b:off_qkv_b + w_b_tot].reshape(Bd, H_b, dh)
            k_s = Ps[:, off_qkv_b + w_b_tot:off_qkv_b + 2 * w_b_tot]
            v_s = Ps[:, off_qkv_b + 2 * w_b_tot:off_qkv_b + 3 * w_b_tot]
            z_s = Ps[:, off_z_b:off_z_b + w_b_tot].reshape(Bd, H_b, dh)
            lbs = _sb_sample(q_s, z_s, sb_bias[j], cache_sb_k, cache_sb_v, page_table, j)
            xs, xn_s = _outproj([las, lbs.reshape(Bd, w_b_tot)], w_out_bf, xs, nw_next)
            sk.append(k_s.reshape(Bd, 1, H_b, dh)); sv.append(v_s.reshape(Bd, 1, H_b, dh))
            sS.append(S_s)
            scg.append(jnp.concatenate([state_gdn_conv[j][:, 1:], u[:, None, :]], axis=1))
        else:
            w_bf = w_in_odd[j].astype(BF16)
            w_out_bf = w_out_odd[j].astype(BF16)
            w_ax_bf = jnp.concatenate([lru_w_a[j], lru_w_x[j]], axis=-1).astype(BF16)
            lru_args = (lru_conv_w[j], lru_conv_b[j], w_ax_bf, lru_b_a[j], lru_b_x[j], lru_lambda[j])
            P = _inproj(xn_p, w_bf, 2 * w_c)
            P3 = P.reshape(B, L, 2 * w_c)
            y, h_last = _lru_prompt(P3, *lru_args, n_meta, w_c)
            xp, xn_p = _outproj([y.reshape(B * L, w_c)], w_out_bf, xp, nw_next)
            ph.append(h_last.reshape(B, w_c)); pcl.append(P3[:, L - (lru_conv_w.shape[1] - 1):, :w_c])
            Ps = _inproj(xn_s, w_bf, 2 * w_c)
            ys, h_s = _lru_sample(Ps, jnp.swapaxes(state_lru_conv[j], 0, 1), *lru_args, state_lru[j], w_c)
            xs, xn_s = _outproj([ys], w_out_bf, xs, nw_next)
            sh.append(h_s)
            scl.append(jnp.concatenate([state_lru_conv[j][:, 1:], Ps[:, None, :w_c]], axis=1))

    y_prompt = _final_norm(xp.reshape(B, L, D), final_norm_w, n_meta)
    y_sample = _final_norm(xs.reshape(1, Bd, D), final_norm_w, 0).reshape(Bd, 1, D)
    st = jnp.stack
    kv5 = (n_even, B, L, H_b, dh)
    return (y_prompt, y_sample, k_slab.reshape(kv5), v_slab.reshape(kv5), st(pS), st(pcg), st(ph), st(pcl),
            st(sk), st(sv), st(sS), st(scg), st(sh), st(scl))
```

```python
import functools
import math

import jax
import jax.numpy as jnp
from jax import lax
from jax.experimental import pallas as pl
from jax.experimental.pallas import tpu as pltpu

EPS = 1e-6
RG_C = 8.0
GDN_CHUNK = 64
SB_BLOCK = 128
SB_GROUP = 256
LRU_CHUNK = 256
LRU_CHUNKS_PER_ITER = 2
GDN_HEADS_PER_STEP = 4
GDN_CHUNKS_PER_ITER = 4
SB_PAGES_PER_STEP = 8
F32 = jnp.float32
BF16 = jnp.bfloat16
V7X_VMEM_LIMIT_BYTES = 56 * 1024 * 1024
NEG_BIG = -1e30


def _cparams(*dims):
    return pltpu.CompilerParams(dimension_semantics=dims, vmem_limit_bytes=V7X_VMEM_LIMIT_BYTES)


def _row_tile(n, cap, mult=16):
    best = None
    for t in range(mult, min(n, cap) + 1, mult):
        if n % t == 0:
            best = t
    return best if best is not None else n


def _sigmoid(x):
    return 1.0 / (1.0 + jnp.exp(-x))


def _silu(x):
    return x * _sigmoid(x)


def _softplus(x):
    return jnp.maximum(x, 0.0) + jnp.log1p(jnp.exp(-jnp.abs(x)))


def _dot(a, b):
    return jnp.dot(a, b, preferred_element_type=F32)


def _dot_nt(a, b):
    return lax.dot_general(a, b, (((1,), (1,)), ((), ())), preferred_element_type=F32)


def _dot_tn(a, b):
    return lax.dot_general(a, b, (((0,), (0,)), ((), ())), preferred_element_type=F32)


def _shift_down(x, k, fill):
    if k % 8 == 0:
        return jnp.concatenate([jnp.full((k, x.shape[1]), fill, x.dtype), x[:x.shape[0] - k]], axis=0)
    rolled = pltpu.roll(x, k, 0)
    rows = lax.broadcasted_iota(jnp.int32, x.shape, 0)
    return jnp.where(rows >= k, rolled, fill)


def _rmsnorm_bf16(xf, nw):
    ms = jnp.mean(xf * xf, axis=-1, keepdims=True)
    return (xf * lax.rsqrt(ms + EPS) * nw).astype(BF16)


def _norm_kernel(x_ref, nw_ref, o_ref):
    o_ref[...] = _rmsnorm_bf16(x_ref[...], nw_ref[...])


def _norm(x, nw, tm_cap=128):
    T, D = x.shape
    tm = _row_tile(T, tm_cap)
    return pl.pallas_call(
        _norm_kernel,
        grid=(T // tm,),
        in_specs=[pl.BlockSpec((tm, D), lambda i: (i, 0)), pl.BlockSpec((1, D), lambda i: (0, 0))],
        out_specs=pl.BlockSpec((tm, D), lambda i: (i, 0)),
        out_shape=jax.ShapeDtypeStruct((T, D), BF16),
        compiler_params=_cparams("parallel"),
        name="norm",
    )(x, nw.reshape(1, D))


def _embed_kernel(x_ref, meta_ref, nw_ref, xp_ref, xn_ref, *, n_meta, rc):
    nw = nw_ref[...]
    meta = meta_ref[...]
    xp_ref[0, 0:n_meta, :] = meta
    xn_ref[0, 0:n_meta, :] = _rmsnorm_bf16(meta, nw)

    def body(c, carry):
        r = pl.multiple_of(c * rc, rc)
        ro = pl.multiple_of(n_meta + c * rc, 16)
        xf = x_ref[0, pl.ds(r, rc), :]
        xp_ref[0, pl.ds(ro, rc), :] = xf
        xn_ref[0, pl.ds(ro, rc), :] = _rmsnorm_bf16(xf, nw)
        return carry

    lax.fori_loop(0, x_ref.shape[1] // rc, body, 0)


def _embed(x, meta, nw):
    B, S, D = x.shape
    n_meta = meta.shape[0]
    L = n_meta + S
    rc = _row_tile(S, 128)
    full = pl.BlockSpec((1, L, D), lambda b: (b, 0, 0))
    return pl.pallas_call(
        functools.partial(_embed_kernel, n_meta=n_meta, rc=rc),
        grid=(B,),
        in_specs=[pl.BlockSpec((1, S, D), lambda b: (b, 0, 0)),
                  pl.BlockSpec((n_meta, D), lambda b: (0, 0)),
                  pl.BlockSpec((1, D), lambda b: (0, 0))],
        out_specs=[full, full],
        out_shape=[jax.ShapeDtypeStruct((B, L, D), F32), jax.ShapeDtypeStruct((B, L, D), BF16)],
        compiler_params=_cparams("parallel"),
        name="embed",
    )(x, meta, nw.reshape(1, D))


def _inproj_kernel(*refs, has_small):
    if has_small:
        x_ref, w_ref, ws_ref, o_ref, os_ref = refs

        @pl.when(pl.program_id(1) == 0)
        def _():
            os_ref[...] = _dot(x_ref[...], ws_ref[...].astype(BF16))
    else:
        x_ref, w_ref, o_ref = refs
    o_ref[...] = _dot(x_ref[...], w_ref[...].astype(BF16))


def _inproj(xn, w_all, layer, n_main, w_small_bf=None, tn=512, tm_cap=2064):
    T, D = xn.shape
    tm = _row_tile(T, tm_cap)
    has_small = w_small_bf is not None
    in_specs = [pl.BlockSpec((tm, D), lambda i, j: (i, 0)),
                pl.BlockSpec((None, D, tn), lambda i, j: (layer, 0, j))]
    out_shape = [jax.ShapeDtypeStruct((T, n_main), F32)]
    out_specs = [pl.BlockSpec((tm, tn), lambda i, j: (i, j))]
    args = [xn, w_all]
    if has_small:
        ns = w_small_bf.shape[1]
        in_specs.append(pl.BlockSpec((D, ns), lambda i, j: (0, 0)))
        out_shape.append(jax.ShapeDtypeStruct((T, ns), F32))
        out_specs.append(pl.BlockSpec((tm, ns), lambda i, j: (i, 0)))
        args.append(w_small_bf)
    out = pl.pallas_call(
        functools.partial(_inproj_kernel, has_small=has_small),
        grid=(T // tm, n_main // tn),
        in_specs=in_specs, out_specs=out_specs, out_shape=out_shape,
        compiler_params=_cparams("parallel", "arbitrary"),
        name="inproj",
    )(*args)
    return out if has_small else out[0]


def _outproj_kernel(*refs, n_parts):
    lhs_refs = refs[:n_parts]
    w_ref, x_ref, nw_ref, o_ref, on_ref = refs[n_parts:]
    acc = x_ref[...]
    k0 = 0
    for lr in lhs_refs:
        kk = lr.shape[1]
        acc = acc + _dot(lr[...], w_ref[k0:k0 + kk, :])
        k0 += kk
    o_ref[...] = acc
    on_ref[...] = _rmsnorm_bf16(acc, nw_ref[...])


def _outproj(lhs_parts, w_bf, x, nw_next, tm_cap=384):
    T, D = x.shape
    tm = _row_tile(T, tm_cap)
    row = pl.BlockSpec((tm, D), lambda i: (i, 0))
    in_specs = [pl.BlockSpec((tm, p.shape[1]), lambda i: (i, 0)) for p in lhs_parts]
    in_specs += [pl.BlockSpec(w_bf.shape, lambda i: (0, 0)), row, pl.BlockSpec((1, D), lambda i: (0, 0))]
    return pl.pallas_call(
        functools.partial(_outproj_kernel, n_parts=len(lhs_parts)),
        grid=(T // tm,),
        in_specs=in_specs,
        out_specs=[row, row],
        out_shape=[jax.ShapeDtypeStruct((T, D), F32), jax.ShapeDtypeStruct((T, D), BF16)],
        compiler_params=_cparams("parallel"),
        name="outproj",
    )(*lhs_parts, w_bf, x, nw_next.reshape(1, D))


def _final_norm_kernel(x_ref, w_ref, o_ref, *, n_skip, tr):
    t = pl.program_id(1)
    r = pl.multiple_of(n_skip + t * tr, 8)
    xf = x_ref[0, pl.ds(r, tr), :]
    ms = jnp.mean(xf * xf, axis=-1, keepdims=True)
    o_ref[0] = xf * lax.rsqrt(ms + EPS) * w_ref[...]


def _final_norm(x, w, n_skip):
    B, L, D = x.shape
    Lo = L - n_skip
    tr = _row_tile(Lo, 512, 8)
    return pl.pallas_call(
        functools.partial(_final_norm_kernel, n_skip=n_skip, tr=tr),
        grid=(B, Lo // tr),
        in_specs=[pl.BlockSpec((1, L, D), lambda b, t: (b, 0, 0)),
                  pl.BlockSpec((1, D), lambda b, t: (0, 0))],
        out_specs=pl.BlockSpec((1, tr, D), lambda b, t: (b, t, 0)),
        out_shape=jax.ShapeDtypeStruct((B, Lo, D), F32),
        compiler_params=_cparams("parallel", "arbitrary"),
        name="final_norm",
    )(x, w.reshape(1, D))


def _conv_rows(win, cw, n):
    kw = cw.shape[0]
    out = win[8:8 + n] * cw[kw - 1:kw, :]
    for tau in range(1, kw):
        out = out + pltpu.roll(win, tau, 0)[8:8 + n] * cw[kw - 1 - tau:kw - tau, :]
    return out


def _conv_at(ref, cw, r0, n, lanes, scr):
    kw = cw.shape[0]
    scr[...] = ref[0, pl.ds(pl.multiple_of(r0 - 8, 8), n + 8), lanes]
    out = scr[8:8 + n, :] * cw[kw - 1:kw, :]
    for tau in range(1, kw):
        out = out + scr[8 - tau:8 - tau + n, :] * cw[kw - 1 - tau:kw - tau, :]
    return out


def _gdn_prompt_kernel(alog_ref, dtb_ref, q_ref, k_ref, v_ref, cwq_ref, cwk_ref, cwv_ref,
                       braw_ref, araw_ref, gnw_ref, z_ref, o_ref, s_ref, S_scr, win_scr,
                       *, n_meta, n_chunks, dk, hg):
    h0 = pl.program_id(1) * hg
    C = GDN_CHUNK
    a_coefs = [-jnp.exp(jnp.full((1, 1), alog_ref[h0 + hh], F32)) for hh in range(hg)]
    dt_bs = [dtb_ref[h0 + hh] for hh in range(hg)]
    ii = lax.broadcasted_iota(jnp.int32, (C, C), 0)
    jj = lax.broadcasted_iota(jnp.int32, (C, C), 1)
    eye = (ii == jj).astype(F32)
    gnw = gnw_ref[...]

    def conv(ref, cw_ref, r0, first, lanes, slot):
        if first:
            win = jnp.concatenate([jnp.zeros((8, dk), F32), ref[0, 0:C, lanes]], axis=0)
            return _conv_rows(win, cw_ref[:, lanes], C)
        return _conv_at(ref, cw_ref[:, lanes], r0, C, lanes, win_scr.at[slot])

    def solve_stage(probs):
        qs, ks, vs, gates = [], [], [], []
        for p, (hh, r0, crow, first) in enumerate(probs):
            lanes = slice(hh * dk, (hh + 1) * dk)
            q = _silu(conv(q_ref, cwq_ref, r0, first, lanes, 3 * p))
            k = _silu(conv(k_ref, cwk_ref, r0, first, lanes, 3 * p + 1))
            v = _silu(conv(v_ref, cwv_ref, r0, first, lanes, 3 * p + 2))
            qs.append(q * (lax.rsqrt(jnp.sum(q * q, axis=-1, keepdims=True) + EPS) * (dk ** -0.5)))
            ks.append(k * lax.rsqrt(jnp.sum(k * k, axis=-1, keepdims=True) + EPS))
            vs.append(v)
            beta_row = _sigmoid(braw_ref[0, hh, pl.ds(crow, 1), :])
            g_row = a_coefs[hh] * _softplus(araw_ref[0, hh, pl.ds(crow, 1), :] + dt_bs[hh])
            if first:
                valid = lax.broadcasted_iota(jnp.int32, (1, C), 1) < n_meta
                beta_row = jnp.where(valid, beta_row, 0.0)
                g_row = jnp.where(valid, g_row, 0.0)
            g_rb = jnp.broadcast_to(g_row, (C, C))
            gc_col = jnp.sum(jnp.where(jj <= ii, g_rb, 0.0), axis=1, keepdims=True)
            g_col = jnp.sum(jnp.where(jj == ii, g_rb, 0.0), axis=1, keepdims=True)
            gc_row = jnp.sum(jnp.where(ii <= jj, jnp.broadcast_to(g_col, (C, C)), 0.0),
                             axis=0, keepdims=True)
            beta_col = jnp.sum(jnp.where(jj == ii, jnp.broadcast_to(beta_row, (C, C)), 0.0),
                               axis=1, keepdims=True)
            g_last = jnp.sum(g_row, axis=1, keepdims=True)
            decay = jnp.exp(jnp.where(ii >= jj, gc_col - gc_row, NEG_BIG))
            gates.append((beta_col, gc_col, g_last, decay, jnp.exp(gc_col)))
        n = len(probs)
        kbs = [ks[p] * gates[p][0] for p in range(n)]
        Xs = [_dot_nt(jnp.concatenate([qs[p], kbs[p]], axis=0).astype(BF16), ks[p].astype(BF16))
              for p in range(n)]
        qks = [Xs[p][:C] * gates[p][3] for p in range(n)]
        Ns = [-jnp.where(ii > jj, Xs[p][C:] * gates[p][3], 0.0) for p in range(n)]
        Tinvs = [eye + N for N in Ns]
        for _ in range(max(1, int(math.ceil(math.log2(C))) - 1)):
            Ns = [_dot(N.astype(BF16), N.astype(BF16)) for N in Ns]
            Tinvs = [Tinvs[p] + _dot(Tinvs[p].astype(BF16), Ns[p].astype(BF16)) for p in range(n)]
        sols = [_dot(Tinvs[p].astype(BF16),
                     jnp.concatenate([vs[p] * gates[p][0], kbs[p] * gates[p][4]], axis=1).astype(BF16))
                for p in range(n)]
        dv = vs[0].shape[1]
        return [(sols[p][:, :dv], sols[p][:, dv:], qks[p], qs[p] * gates[p][4],
                 ks[p] * jnp.exp(gates[p][2] - gates[p][1]), jnp.exp(gates[p][2])) for p in range(n)]

    def state_stage(items, r0, n_store):
        Ss = [S_scr[hh] for hh in range(hg)]
        WQs = [_dot(jnp.concatenate([items[hh][1], items[hh][3]], axis=0).astype(BF16), Ss[hh].astype(BF16))
               for hh in range(hg)]
        v_news = [items[hh][0] - WQs[hh][:C] for hh in range(hg)]
        os_ = [WQs[hh][C:] + _dot(items[hh][2].astype(BF16), v_news[hh].astype(BF16)) for hh in range(hg)]
        S_news = [Ss[hh] * items[hh][5] + _dot_tn(items[hh][4].astype(BF16), v_news[hh].astype(BF16))
                  for hh in range(hg)]
        for hh in range(hg):
            lanes = slice(hh * dk, (hh + 1) * dk)
            S_scr[hh] = S_news[hh]
            o = os_[hh]
            on = o * lax.rsqrt(jnp.mean(o * o, axis=-1, keepdims=True) + EPS) * gnw
            o_ref[0, pl.ds(r0, n_store), lanes] = (
                on[:n_store] * _silu(z_ref[0, pl.ds(r0, n_store), lanes])).astype(o_ref.dtype)

    S_scr[...] = jnp.zeros(S_scr.shape, F32)
    state_stage(solve_stage([(hh, 0, 0, True) for hh in range(hg)]), 0, n_meta)

    cb = GDN_CHUNKS_PER_ITER if n_chunks % GDN_CHUNKS_PER_ITER == 0 else 1

    def body(t, carry):
        r0s = [pl.multiple_of(n_meta + (t * cb + cc) * C, 16) for cc in range(cb)]
        items = solve_stage([(hh, r0s[cc], t * cb + cc + 1, False) for cc in range(cb) for hh in range(hg)])
        for cc in range(cb):
            state_stage(items[cc * hg:(cc + 1) * hg], r0s[cc], C)
        return carry

    lax.fori_loop(0, n_chunks // cb, body, 0)
    s_ref[0] = S_scr[...]


def _gdn_prompt(P, gates_c, conv_w, a_log, dt_bias, gn_w, n_meta, H, dk, dv, off_z):
    B, L, _ = P.shape
    n_chunks = (L - n_meta) // GDN_CHUNK
    hg = GDN_HEADS_PER_STEP if H % GDN_HEADS_PER_STEP == 0 else 1
    ng = H // hg
    zb0 = off_z // (hg * dv)
    cw = conv_w.shape[0]
    spec_col = lambda off: pl.BlockSpec((1, L, hg * dk), lambda b, g: (b, 0, off + g))
    spec_cw = lambda off: pl.BlockSpec((cw, hg * dk), lambda b, g: (0, off + g))
    smem = pl.BlockSpec(memory_space=pltpu.SMEM)
    nc1 = gates_c.shape[2]
    return pl.pallas_call(
        functools.partial(_gdn_prompt_kernel, n_meta=n_meta, n_chunks=n_chunks, dk=dk, hg=hg),
        grid=(B, ng),
        in_specs=[smem, smem, spec_col(0), spec_col(ng), spec_col(2 * ng),
                  spec_cw(0), spec_cw(ng), spec_cw(2 * ng),
                  pl.BlockSpec((1, hg, nc1, GDN_CHUNK), lambda b, g: (b, g, 0, 0)),
                  pl.BlockSpec((1, hg, nc1, GDN_CHUNK), lambda b, g: (b, ng + g, 0, 0)),
                  pl.BlockSpec((1, dv), lambda b, g: (0, 0)),
                  pl.BlockSpec((1, L, hg * dv), lambda b, g: (b, 0, zb0 + g))],
        out_specs=[pl.BlockSpec((1, L, hg * dv), lambda b, g: (b, 0, g)),
                   pl.BlockSpec((1, hg, dk, dv), lambda b, g: (b, g, 0, 0))],
        out_shape=[jax.ShapeDtypeStruct((B, L, H * dv), BF16),
                   jax.ShapeDtypeStruct((B, H, dk, dv), F32)],
        scratch_shapes=[pltpu.VMEM((hg, dk, dv), F32),
                        pltpu.VMEM((3 * hg * GDN_CHUNKS_PER_ITER, GDN_CHUNK + 8, dk), F32)],
        compiler_params=_cparams("parallel", "arbitrary"),
        name="gdn_prompt",
    )(a_log, dt_bias, P, P, P, conv_w, conv_w, conv_w, gates_c, gates_c,
      gn_w.reshape(1, dv), P)


def _sb_prompt_kernel(*refs, layer, n_fill, **kw):
    ins = refs[:5]
    o_ref, ko_ref, vo_ref = refs[-3:]
    if n_fill > 1:
        for other in range(n_fill):
            if other != layer:
                ko_ref[other] = jnp.zeros(ko_ref.shape[1:], F32)
                vo_ref[other] = jnp.zeros(vo_ref.shape[1:], F32)
        ko_ref, vo_ref = ko_ref.at[layer], vo_ref.at[layer]
    _sb_prompt_body(pl.program_id(1), *ins, o_ref, ko_ref, vo_ref, **kw)


def _sb_prompt_body(h, bias_ref, q_ref, k_ref, v_ref, z_ref, o_ref, ko_ref, vo_ref,
                    *, n_meta, n_qb, qb, dh):
    MB = SB_BLOCK
    bias = bias_ref[h]
    scale = dh ** -0.5

    def strict_lower(n):
        return jnp.where(lax.broadcasted_iota(jnp.int32, (n, n), 0)
                         > lax.broadcasted_iota(jnp.int32, (n, n), 1), -1.0, 0.0).astype(BF16)

    gw_main = min(qb, SB_GROUP)
    U_main = strict_lower(gw_main)
    U_meta = U_main if gw_main == MB else strict_lower(MB)

    ko_ref[...] = k_ref[...]
    vo_ref[...] = v_ref[...]

    def tile(q_bf, ks, kw, mode, carry):
        acc, c = carry
        rows = q_bf.shape[0]
        kblk = k_ref[0, pl.ds(ks, kw), :].astype(BF16)
        vblk = v_ref[0, pl.ds(ks, kw), :].astype(BF16)
        z = _dot_nt(q_bf, kblk) + bias
        sp = jnp.maximum(z, 0.0) + jnp.log(1.0 + jnp.exp(-jnp.abs(z)))
        ls = z - sp
        if mode is None:
            vis = None
        else:
            col = lax.broadcasted_iota(jnp.int32, (rows, kw), 1)
            vis = (col < lax.broadcasted_iota(jnp.int32, (rows, kw), 0)) if mode == "diag" else (col < n_meta)
            sp = jnp.where(vis, sp, 0.0)
        gw = min(kw, SB_GROUP)
        U = U_main if gw == gw_main else U_meta
        sp_bf = sp.astype(BF16)
        a_parts = [None] * (kw // gw)
        for g in range(kw // gw - 1, -1, -1):
            sl = slice(g * gw, (g + 1) * gw)
            E = _dot(sp_bf[:, sl], U)
            a = jnp.exp(ls[:, sl] + E + c)
            a_parts[g] = a if vis is None else jnp.where(vis[:, sl], a, 0.0)
            c = c + (E[:, 0:1] - sp[:, g * gw:g * gw + 1])
        a_all = a_parts[0] if len(a_parts) == 1 else jnp.concatenate(a_parts, axis=1)
        return acc + _dot(a_all.astype(BF16), vblk), c

    def zero(rows):
        return (jnp.zeros((rows, dh), F32), jnp.zeros((rows, 1), F32))

    acc0, _ = tile((q_ref[0, 0:MB, :] * scale).astype(BF16), 0, MB, "diag", zero(MB))
    o_ref[0, 0:n_meta, :] = (acc0[:n_meta] * _silu(z_ref[0, 0:n_meta, :])).astype(o_ref.dtype)

    def qblock(i, carry):
        qs = pl.multiple_of(n_meta + i * qb, 16)
        q_bf = (q_ref[0, pl.ds(qs, qb), :] * scale).astype(BF16)
        st = tile(q_bf, qs, qb, "diag", zero(qb))

        def kstep(t, st):
            ks = pl.multiple_of(n_meta + (i - 1 - t) * qb, 16)
            return tile(q_bf, ks, qb, None, st)

        st = lax.fori_loop(0, i, kstep, st)
        acc, _ = tile(q_bf, 0, MB, "meta", st)
        o_ref[0, pl.ds(qs, qb), :] = (acc * _silu(z_ref[0, pl.ds(qs, qb), :])).astype(o_ref.dtype)
        return carry

    lax.fori_loop(0, n_qb, qblock, 0)


def _sb_prompt(P, sb_bias, n_meta, H, dh, off_q, off_z, layer, n_layers, kv_slabs=None):
    B, L, _ = P.shape
    qb = next(c for c in (512, 256, 128) if (L - n_meta) % c == 0)
    n_qb = (L - n_meta) // qb
    qb0, zb0 = off_q // dh, off_z // dh
    n_fill = n_layers if kv_slabs is None else 1
    col = lambda off: pl.BlockSpec((1, L, dh), lambda b, h: (b, 0, off + h))
    if kv_slabs is None:
        slab = pl.BlockSpec((n_layers, 1, L, dh), lambda b, h: (0, b, 0, h))
    else:
        slab = pl.BlockSpec((None, 1, L, dh), lambda b, h: (layer, b, 0, h))
    in_specs = [pl.BlockSpec(memory_space=pltpu.SMEM), col(qb0), col(qb0 + H), col(qb0 + 2 * H), col(zb0)]
    args = [sb_bias, P, P, P, P]
    aliases = {}
    if kv_slabs is not None:
        in_specs += [pl.BlockSpec(memory_space=pl.ANY)] * 2
        aliases = {len(args): 1, len(args) + 1: 2}
        args += list(kv_slabs)
    slab_shape = jax.ShapeDtypeStruct((n_layers, B, L, H * dh), F32)
    return pl.pallas_call(
        functools.partial(_sb_prompt_kernel, layer=layer, n_fill=n_fill, n_meta=n_meta, n_qb=n_qb, qb=qb, dh=dh),
        grid=(B, H),
        in_specs=in_specs,
        out_specs=[col(0), slab, slab],
        out_shape=[jax.ShapeDtypeStruct((B, L, H * dh), BF16), slab_shape, slab_shape],
        input_output_aliases=aliases,
        compiler_params=_cparams("parallel", "arbitrary"),
        name="sb_prompt",
    )(*args)


def _lru_gates(xc, w_ref, ba, bx, ls_lam):
    bw = xc.shape[1]
    pre = _dot(xc.astype(BF16), w_ref[0])
    r = _sigmoid(pre[:, :bw] + ba)
    i = _sigmoid(pre[:, bw:] + bx)
    log_a = RG_C * r * ls_lam
    a = jnp.exp(log_a)
    om = 1.0 - a * a
    root = jnp.where(om > 0.0, om * lax.rsqrt(om), 0.0)
    return a, root * (i * xc)


def _lru_prompt_kernel(x_ref, g_ref, cw_ref, cb_ref, w_ref, ba_ref, bx_ref, lam_ref,
                       y_ref, hl_ref, h_scr, win_scr, ab_scr, hin_scr, *, n_meta, n_chunks):
    TC = LRU_CHUNK
    cw = cw_ref[...]
    cb = cb_ref[...]
    ba, bx = ba_ref[...], bx_ref[...]
    ls_lam = -_softplus(-lam_ref[...])
    bw = x_ref.shape[2]

    def scan_chunk(xc, n, r0, h_prev):
        a, b = _lru_gates(xc + cb, w_ref, ba, bx, ls_lam)
        k = 1
        while k < n:
            a_s = _shift_down(a, k, 1.0)
            b_s = _shift_down(b, k, 0.0)
            b = a * b_s + b
            a = a * a_s
            k *= 2
        hrows = a * h_prev + b
        y_ref[0, pl.ds(r0, n), :] = (hrows * _silu(g_ref[0, pl.ds(r0, n), :])).astype(y_ref.dtype)
        return hrows[n - 1:n, :]

    def scan_prep(r0, slot):
        xc = _conv_at(x_ref, cw, r0, TC, slice(None), win_scr.at[slot])
        a, b = _lru_gates(xc + cb, w_ref, ba, bx, ls_lam)
        sub = lax.broadcasted_iota(jnp.int32, (TC, bw), 0) % 8
        for k in (1, 2, 4):
            keep = sub >= k
            a_s = jnp.where(keep, pltpu.roll(a, k, 0), 1.0)
            b_s = jnp.where(keep, pltpu.roll(b, k, 0), 0.0)
            b = a * b_s + b
            a = a * a_s
        ng = TC // 8
        ab_scr[slot, 0] = a
        ab_scr[slot, 1] = b
        at = ab_scr[slot, 0, pl.ds(7, ng, stride=8), :]
        bt = ab_scr[slot, 1, pl.ds(7, ng, stride=8), :]
        k = 1
        while k < ng:
            at_s = _shift_down(at, k, 1.0)
            bt_s = _shift_down(bt, k, 0.0)
            bt = at * bt_s + bt
            at = at * at_s
            k *= 2
        return a, b, _shift_down(at, 1, 1.0), _shift_down(bt, 1, 0.0)

    def scan_finish(prep, r0, slot, h_prev):
        a, b, atx, btx = prep
        ng = TC // 8
        hin_scr[slot] = atx * h_prev + btx
        hrows = jnp.concatenate(
            [a[8 * g:8 * g + 8] * hin_scr[slot, g:g + 1, :] + b[8 * g:8 * g + 8] for g in range(ng)], axis=0)
        y_ref[0, pl.ds(r0, TC), :] = (hrows * _silu(g_ref[0, pl.ds(r0, TC), :])).astype(y_ref.dtype)
        return hrows[TC - 1:TC, :]

    win0 = jnp.concatenate([jnp.zeros((8, bw), F32), x_ref[0, 0:n_meta, :]], axis=0)
    h_scr[...] = scan_chunk(_conv_rows(win0, cw, n_meta), n_meta, 0, jnp.zeros((1, bw), F32))

    cb_n = LRU_CHUNKS_PER_ITER if n_chunks % LRU_CHUNKS_PER_ITER == 0 else 1

    def body(t, carry):
        r0s = [pl.multiple_of(n_meta + (t * cb_n + cc) * TC, 16) for cc in range(cb_n)]
        preps = [scan_prep(r0s[cc], cc) for cc in range(cb_n)]
        h = h_scr[...]
        for cc in range(cb_n):
            h = scan_finish(preps[cc], r0s[cc], cc, h)
        h_scr[...] = h
        return carry

    lax.fori_loop(0, n_chunks // cb_n, body, 0)
    hl_ref[0] = h_scr[...]


def _lru_prompt(P, conv_w, conv_b, w_ax_bf, b_a, b_x, lam, n_meta, w_c):
    B, L, _ = P.shape
    hc, bw, _ = w_ax_bf.shape
    n_chunks = (L - n_meta) // LRU_CHUNK
    kw = conv_w.shape[0]
    vec = lambda a: a.reshape(1, w_c)
    vspec = pl.BlockSpec((1, bw), lambda b, h: (0, h))
    return pl.pallas_call(
        functools.partial(_lru_prompt_kernel, n_meta=n_meta, n_chunks=n_chunks),
        grid=(B, hc),
        in_specs=[pl.BlockSpec((1, L, bw), lambda b, h: (b, 0, h)),
                  pl.BlockSpec((1, L, bw), lambda b, h: (b, 0, hc + h)),
                  pl.BlockSpec((kw, bw), lambda b, h: (0, h)),
                  vspec,
                  pl.BlockSpec((1, bw, 2 * bw), lambda b, h: (h, 0, 0)),
                  vspec, vspec, vspec],
        out_specs=[pl.BlockSpec((1, L, bw), lambda b, h: (b, 0, h)),
                   pl.BlockSpec((1, 1, bw), lambda b, h: (b, 0, h))],
        out_shape=[jax.ShapeDtypeStruct((B, L, w_c), BF16),
                   jax.ShapeDtypeStruct((B, 1, w_c), F32)],
        scratch_shapes=[pltpu.VMEM((1, bw), F32),
                        pltpu.VMEM((LRU_CHUNKS_PER_ITER, LRU_CHUNK + 8, bw), F32),
                        pltpu.VMEM((LRU_CHUNKS_PER_ITER, 2, LRU_CHUNK, bw), F32),
                        pltpu.VMEM((LRU_CHUNKS_PER_ITER, LRU_CHUNK // 8, bw), F32)],
        compiler_params=_cparams("parallel", "arbitrary"),
        name="lru_prompt",
    )(P, P, conv_w, vec(conv_b), w_ax_bf, vec(b_a), vec(b_x), vec(lam))


def _gdn_sample_kernel(alog_ref, dtb_ref, u_ref, buf_ref, cw_ref, gates_ref, gnw_ref, z_ref, s_ref,
                       o_ref, so_ref, *, H, dk, dv):
    kw = cw_ref.shape[0]
    xc = u_ref[0] * cw_ref[kw - 1:kw, :]
    for t in range(kw - 1):
        xc = xc + buf_ref[0, t:t + 1, :] * cw_ref[t:t + 1, :]
    xc = _silu(xc)
    gates = gates_ref[0]
    gnw = gnw_ref[...]
    row8 = lax.broadcasted_iota(jnp.int32, (8, dk), 0)
    di = lax.broadcasted_iota(jnp.int32, (dk, dk), 0)
    dj = lax.broadcasted_iota(jnp.int32, (dk, dk), 1)
    for h in range(H):
        q = xc[:, h * dk:(h + 1) * dk]
        k = xc[:, (H + h) * dk:(H + h + 1) * dk]
        v = xc[:, 2 * H * dk + h * dv:2 * H * dk + (h + 1) * dv]
        q = q * (lax.rsqrt(jnp.sum(q * q, axis=-1, keepdims=True) + EPS) * (dk ** -0.5))
        k = k * lax.rsqrt(jnp.sum(k * k, axis=-1, keepdims=True) + EPS)
        beta = _sigmoid(gates[:, h:h + 1])
        g = -jnp.exp(jnp.full((1, 1), alog_ref[h], F32)) * _softplus(gates[:, H + h:H + h + 1] + dtb_ref[h])
        eg = jnp.exp(g)
        S = s_ref[0, h]
        kq = jnp.where(row8 == 0, jnp.broadcast_to(k, (8, dk)),
                       jnp.where(row8 == 1, jnp.broadcast_to(q * eg, (8, dk)), 0.0))
        R = _dot(kq.astype(BF16), S.astype(BF16))
        v_new = beta * (v - eg * R[0:1])
        o = R[1:2] + jnp.sum(q * k, axis=-1, keepdims=True) * v_new
        k_col = jnp.sum(jnp.where(di == dj, jnp.broadcast_to(k, (dk, dk)), 0.0),
                        axis=1, keepdims=True)
        so_ref[0, h] = S * eg + k_col * v_new
        on = o * lax.rsqrt(jnp.mean(o * o, axis=-1, keepdims=True) + EPS) * gnw
        o_ref[0, :, h * dv:(h + 1) * dv] = (on * _silu(z_ref[0, :, h * dv:(h + 1) * dv])).astype(o_ref.dtype)


def _gdn_sample(u, buf, conv_w, gates, a_log, dt_bias, gn_w, z_a, S, layer, H, dk, dv):
    Bd, C = u.shape
    kw = conv_w.shape[0]
    smem = pl.BlockSpec(memory_space=pltpu.SMEM)
    row = lambda n: pl.BlockSpec((1, 1, n), lambda b: (b, 0, 0))
    o, s_new = pl.pallas_call(
        functools.partial(_gdn_sample_kernel, H=H, dk=dk, dv=dv),
        grid=(Bd,),
        in_specs=[smem, smem, row(C),
                  pl.BlockSpec((1, kw - 1, C), lambda b: (b, 0, 0)),
                  pl.BlockSpec((kw, C), lambda b: (0, 0)),
                  row(2 * H),
                  pl.BlockSpec((1, dv), lambda b: (0, 0)),
                  row(H * dv),
                  pl.BlockSpec((None, 1, H, dk, dv), lambda b: (layer, b, 0, 0, 0))],
        out_specs=[row(H * dv), pl.BlockSpec((1, H, dk, dv), lambda b: (b, 0, 0, 0))],
        out_shape=[jax.ShapeDtypeStruct((Bd, 1, H * dv), BF16),
                   jax.ShapeDtypeStruct((Bd, H, dk, dv), F32)],
        compiler_params=_cparams("parallel"),
        name="gdn_sample",
    )(a_log, dt_bias, u.reshape(Bd, 1, C), buf, conv_w, gates.reshape(Bd, 1, 2 * H),
      gn_w.reshape(1, dv), z_a.reshape(Bd, 1, H * dv), S)
    return o.reshape(Bd, H * dv), s_new


def _sb_sample_kernel(pt_ref, q_ref, bias_ref, z_ref, *refs, H, dh, page, pps, n_steps):
    k_refs = refs[:pps]
    v_refs = refs[pps:2 * pps]
    o_ref, acc_scr, c_scr = refs[2 * pps:]
    p = pl.program_id(1)
    LW = 128
    nv = (page * H) // LW

    @pl.when(p == 0)
    def _():
        acc_scr[...] = jnp.zeros(acc_scr.shape, F32)
        c_scr[...] = jnp.zeros(c_scr.shape, F32)

    q_bf = (q_ref[0] * (dh ** -0.5)).astype(BF16)
    bias = bias_ref[...]
    lane = lax.broadcasted_iota(jnp.int32, (H, LW), 1)
    sub = lax.broadcasted_iota(jnp.int32, (H, LW), 0)
    valid = (lane % H) == sub
    ui = lax.broadcasted_iota(jnp.int32, (LW, LW), 0)
    uj = lax.broadcasted_iota(jnp.int32, (LW, LW), 1)
    U = (ui > uj).astype(BF16)

    ls_all, l1m_all = [], []
    for s in range(pps):
        kp = k_refs[s][0, 0].astype(BF16)
        z = _dot_nt(q_bf, kp) + bias
        sp = jnp.maximum(z, 0.0) + jnp.log(1.0 + jnp.exp(-jnp.abs(z)))
        ls_all.append(z - sp)
        l1m_all += [jnp.where(valid, -sp[:, i * LW:(i + 1) * LW], 0.0) for i in range(nv)]
    L = jnp.concatenate(l1m_all, axis=0)
    hi = L.astype(BF16)
    lo = (L - hi.astype(F32)).astype(BF16)
    n_l = pps * nv * H
    E2 = _dot(jnp.concatenate([hi, lo], axis=0), U)
    E = E2[:n_l] + E2[n_l:]
    tot = E[:, 0:1] + L[:, 0:1]
    acc = acc_scr[...]
    c = c_scr[...]
    for s in range(pps):
        a_parts = [None] * nv
        for i in range(nv - 1, -1, -1):
            g = s * nv + i
            loga = ls_all[s][:, i * LW:(i + 1) * LW] + E[g * H:(g + 1) * H] + c
            a_parts[i] = jnp.where(valid, jnp.exp(loga), 0.0)
            c = c + tot[g * H:(g + 1) * H]
        a = jnp.concatenate(a_parts, axis=1).astype(BF16)
        acc = acc + _dot(a, v_refs[s][0, 0].astype(BF16))
    acc_scr[...] = acc
    c_scr[...] = c

    @pl.when(p == n_steps - 1)
    def _():
        o_ref[0] = (acc * _silu(z_ref[0])).astype(o_ref.dtype)


def _sb_sample(q, z_b, sb_bias, cache_k, cache_v, page_table, layer):
    Bd, H, dh = q.shape
    _, n_phys, page, _, _ = cache_k.shape
    n_pages = page_table.shape[1]
    pps = SB_PAGES_PER_STEP if n_pages % SB_PAGES_PER_STEP == 0 else 1
    n_steps = n_pages // pps
    ck = cache_k.reshape(cache_k.shape[0], n_phys, page * H, dh)
    cv = cache_v.reshape(cache_v.shape[0], n_phys, page * H, dh)

    def page_spec(s):
        def imap(b, p, pt):
            return (layer, pt[b * n_pages + (n_pages - 1 - (p * pps + s))], 0, 0)
        return pl.BlockSpec((1, 1, page * H, dh), imap)

    hd = pl.BlockSpec((1, H, dh), lambda b, p, pt: (b, 0, 0))
    grid_spec = pltpu.PrefetchScalarGridSpec(
        num_scalar_prefetch=1,
        grid=(Bd, n_steps),
        in_specs=[hd, pl.BlockSpec((H, 1), lambda b, p, pt: (0, 0)), hd]
                 + [page_spec(s) for s in range(pps)] * 2,
        out_specs=hd,
        scratch_shapes=[pltpu.VMEM((H, dh), F32), pltpu.VMEM((H, 1), F32)],
    )
    return pl.pallas_call(
        functools.partial(_sb_sample_kernel, H=H, dh=dh, page=page, pps=pps, n_steps=n_steps),
        grid_spec=grid_spec,
        out_shape=jax.ShapeDtypeStruct((Bd, H, dh), BF16),
        compiler_params=_cparams("parallel", "arbitrary"),
        name="sb_sample",
    )(page_table.reshape(-1), q, sb_bias.reshape(H, 1), z_b, *([ck] * pps), *([cv] * pps))


def _lru_sample_kernel(x_ref, g_ref, buf_ref, cw_ref, cb_ref, w_ref, ba_ref, bx_ref, lam_ref, h0_ref,
                       y_ref, h_ref):
    kw = cw_ref.shape[0]
    xc = x_ref[...] * cw_ref[kw - 1:kw, :] + cb_ref[...]
    for t in range(kw - 1):
        xc = xc + buf_ref[t] * cw_ref[t:t + 1, :]
    ls_lam = -_softplus(-lam_ref[...])
    a, b = _lru_gates(xc, w_ref, ba_ref[...], bx_ref[...], ls_lam)
    hn = a * h0_ref[...] + b
    h_ref[...] = hn
    y_ref[...] = (hn * _silu(g_ref[...])).astype(y_ref.dtype)


def _lru_sample(P, buf_t, conv_w, conv_b, w_ax_bf, b_a, b_x, lam, h0, w_c):
    Bd = P.shape[0]
    hc, bw, _ = w_ax_bf.shape
    kw = conv_w.shape[0]
    vec = lambda a: a.reshape(1, w_c)
    vspec = pl.BlockSpec((1, bw), lambda h: (0, h))
    mspec = lambda off: pl.BlockSpec((Bd, bw), lambda h: (0, off + h))
    return pl.pallas_call(
        _lru_sample_kernel,
        grid=(hc,),
        in_specs=[mspec(0), mspec(hc),
                  pl.BlockSpec((kw - 1, Bd, bw), lambda h: (0, 0, h)),
                  pl.BlockSpec((kw, bw), lambda h: (0, h)),
                  vspec,
                  pl.BlockSpec((1, bw, 2 * bw), lambda h: (h, 0, 0)),
                  vspec, vspec, vspec, mspec(0)],
        out_specs=[mspec(0), mspec(0)],
        out_shape=[jax.ShapeDtypeStruct((Bd, w_c), BF16), jax.ShapeDtypeStruct((Bd, w_c), F32)],
        compiler_params=_cparams("parallel"),
        name="lru_sample",
    )(P, P, buf_t, conv_w, vec(conv_b), w_ax_bf, vec(b_a), vec(b_x), vec(lam), h0)


def kernel(x_prompt, x_sample, cache_sb_k, cache_sb_v, state_gdn, state_gdn_conv, state_lru,
           state_lru_conv, page_table, meta_tokens, norm_w, final_norm_w, w_in_even, gdn_conv_w,
           gdn_a_log, gdn_dt_bias, gdn_norm_w, sb_bias, w_out_even, w_in_odd, lru_conv_w, lru_conv_b,
           lru_w_a, lru_b_a, lru_w_x, lru_b_x, lru_lambda, w_out_odd):
    B, seq, D = x_prompt.shape
    Bd = x_sample.shape[0]
    assert x_sample.shape[1] == 1
    n_meta = meta_tokens.shape[0]
    L = n_meta + seq
    depth = norm_w.shape[0]
    _, _, H_a, dk, dv = state_gdn.shape
    _, _, _, H_b, dh = cache_sb_k.shape
    w_a_tot, w_b_tot = H_a * dv, H_b * dh
    qkv_a = 2 * H_a * dk + H_a * dv
    off_qkv_b = qkv_a
    off_z_a = off_qkv_b + 3 * w_b_tot
    off_z_b = off_z_a + w_a_tot
    off_gate = off_z_b + w_b_tot
    w_c = state_lru.shape[2]
    assert dk == dv == dh == 128 and n_meta % 16 == 0 and n_meta <= GDN_CHUNK
    assert seq % LRU_CHUNK == 0 and seq % SB_BLOCK == 0 and seq % GDN_CHUNK == 0
    assert w_in_even.shape[2] == off_gate + 2 * H_a and lru_w_a.shape[2] == 128

    xp, xn_p = _embed(x_prompt, meta_tokens.astype(x_prompt.dtype), norm_w[0])
    xp, xn_p = xp.reshape(B * L, D), xn_p.reshape(B * L, D)
    xs = x_sample.reshape(Bd, D)
    xn_s = _norm(xs, norm_w[0])

    n_chunks = seq // GDN_CHUNK
    n_even = (depth + 1) // 2
    k_slab = v_slab = None
    pS, pcg, ph, pcl = [], [], [], []
    sk, sv, sS, scg, sh, scl = [], [], [], [], [], []
    for layer in range(depth):
        j = layer // 2
        nw_next = norm_w[layer + 1] if layer + 1 < depth else final_norm_w
        if layer % 2 == 0:
            w_gate = w_in_even[j, :, off_gate:]
            w_out_bf = w_out_even[j].astype(BF16)
            P, gates = _inproj(xn_p, w_in_even, j, off_gate, w_gate)
            P3 = P.reshape(B, L, off_gate)
            g3 = jnp.swapaxes(gates.reshape(B, L, 2 * H_a), 1, 2)
            g_meta = jnp.pad(g3[:, :, :n_meta], ((0, 0), (0, 0), (0, GDN_CHUNK - n_meta)))
            gates_c = jnp.concatenate(
                [g_meta[:, :, None, :], g3[:, :, n_meta:].reshape(B, 2 * H_a, n_chunks, GDN_CHUNK)], axis=2)
            la, S_p = _gdn_prompt(P3, gates_c, gdn_conv_w[j], gdn_a_log[j], gdn_dt_bias[j], gdn_norm_w[j],
                                  n_meta, H_a, dk, dv, off_z_a)
            lb, k_slab, v_slab = _sb_prompt(P3, sb_bias[j], n_meta, H_b, dh, off_qkv_b, off_z_b, j, n_even,
                                            None if j == 0 else (k_slab, v_slab))
            xp, xn_p = _outproj([la.reshape(B * L, w_a_tot), lb.reshape(B * L, w_b_tot)], w_out_bf, xp, nw_next)
            pS.append(S_p); pcg.append(P3[:, L - (gdn_conv_w.shape[1] - 1):, :qkv_a])
            Ps, gs = _inproj(xn_s, w_in_even, j, off_gate, w_gate)
            u = Ps[:, :qkv_a]
            las, S_s = _gdn_sample(u, state_gdn_conv[j], gdn_conv_w[j], gs, gdn_a_log[j], gdn_dt_bias[j],
                                   gdn_norm_w[j], Ps[:, off_z_a:off_z_a + w_a_tot], state_gdn, j, H_a, dk, dv)
            q_s = Ps[:, off_qkv_b:off_qkv_b + w_b_tot].reshape(Bd, H_b, dh)
            k_s = Ps[:, off_qkv_b + w_b_tot:off_qkv_b + 2 * w_b_tot]
            v_s = Ps[:, off_qkv_b + 2 * w_b_tot:off_qkv_b + 3 * w_b_tot]
            z_s = Ps[:, off_z_b:off_z_b + w_b_tot].reshape(Bd, H_b, dh)
            lbs = _sb_sample(q_s, z_s, sb_bias[j], cache_sb_k, cache_sb_v, page_table, j)
            xs, xn_s = _outproj([las, lbs.reshape(Bd, w_b_tot)], w_out_bf, xs, nw_next)
            sk.append(k_s.reshape(Bd, 1, H_b, dh)); sv.append(v_s.reshape(Bd, 1, H_b, dh))
            sS.append(S_s)
            scg.append(jnp.concatenate([state_gdn_conv[j][:, 1:], u[:, None, :]], axis=1))
        else:
            w_out_bf = w_out_odd[j].astype(BF16)
            w_ax_bf = jnp.concatenate([lru_w_a[j], lru_w_x[j]], axis=-1).astype(BF16)
            lru_args = (lru_conv_w[j], lru_conv_b[j], w_ax_bf, lru_b_a[j], lru_b_x[j], lru_lambda[j])
            P = _inproj(xn_p, w_in_odd, j, 2 * w_c)
            P3 = P.reshape(B, L, 2 * w_c)
            y, h_last = _lru_prompt(P3, *lru_args, n_meta, w_c)
            xp, xn_p = _outproj([y.reshape(B * L, w_c)], w_out_bf, xp, nw_next)
            ph.append(h_last.reshape(B, w_c)); pcl.append(P3[:, L - (lru_conv_w.shape[1] - 1):, :w_c])
            Ps = _inproj(xn_s, w_in_odd, j, 2 * w_c)
            ys, h_s = _lru_sample(Ps, jnp.swapaxes(state_lru_conv[j], 0, 1), *lru_args, state_lru[j], w_c)
            xs, xn_s = _outproj([ys], w_out_bf, xs, nw_next)
            sh.append(h_s)
            scl.append(jnp.concatenate([state_lru_conv[j][:, 1:], Ps[:, None, :w_c]], axis=1))

    y_prompt = _final_norm(xp.reshape(B, L, D), final_norm_w, n_meta)
    y_sample = _final_norm(xs.reshape(1, Bd, D), final_norm_w, 0).reshape(Bd, 1, D)
    st = jnp.stack
    kv5 = (n_even, B, L, H_b, dh)
    return (y_prompt, y_sample, k_slab.reshape(kv5), v_slab.reshape(kv5), st(pS), st(pcg), st(ph), st(pcl),
            st(sk), st(sv), st(sS), st(scg), st(sh), st(scl))
```

```python
import functools
import math

import jax
import jax.numpy as jnp
from jax import lax
from jax.experimental import pallas as pl
from jax.experimental.pallas import tpu as pltpu

EPS = 1e-6
RG_C = 8.0
GDN_CHUNK = 64
SB_BLOCK = 128
SB_GROUP = 256
LRU_CHUNK = 256
LRU_CHUNKS_PER_ITER = 2
GDN_HEADS_PER_STEP = 4
GDN_CHUNKS_PER_ITER = 4
SB_PAGES_PER_STEP = 16
F32 = jnp.float32
BF16 = jnp.bfloat16
V7X_VMEM_LIMIT_BYTES = 56 * 1024 * 1024
NEG_BIG = -1e30


def _cparams(*dims):
    return pltpu.CompilerParams(dimension_semantics=dims, vmem_limit_bytes=V7X_VMEM_LIMIT_BYTES)


def _row_tile(n, cap, mult=16):
    best = None
    for t in range(mult, min(n, cap) + 1, mult):
        if n % t == 0:
            best = t
    return best if best is not None else n


def _sigmoid(x):
    return 1.0 / (1.0 + jnp.exp(-x))


def _silu(x):
    return x * _sigmoid(x)


def _softplus(x):
    return jnp.maximum(x, 0.0) + jnp.log1p(jnp.exp(-jnp.abs(x)))


def _dot(a, b):
    return jnp.dot(a, b, preferred_element_type=F32)


def _dot_nt(a, b):
    return lax.dot_general(a, b, (((1,), (1,)), ((), ())), preferred_element_type=F32)


def _dot_tn(a, b):
    return lax.dot_general(a, b, (((0,), (0,)), ((), ())), preferred_element_type=F32)


def _shift_down(x, k, fill):
    if k % 8 == 0:
        return jnp.concatenate([jnp.full((k, x.shape[1]), fill, x.dtype), x[:x.shape[0] - k]], axis=0)
    rolled = pltpu.roll(x, k, 0)
    rows = lax.broadcasted_iota(jnp.int32, x.shape, 0)
    return jnp.where(rows >= k, rolled, fill)


def _rmsnorm_bf16(xf, nw):
    ms = jnp.mean(xf * xf, axis=-1, keepdims=True)
    return (xf * lax.rsqrt(ms + EPS) * nw).astype(BF16)


def _norm_kernel(x_ref, nw_ref, o_ref):
    o_ref[...] = _rmsnorm_bf16(x_ref[...], nw_ref[...])


def _norm(x, nw, tm_cap=128):
    T, D = x.shape
    tm = _row_tile(T, tm_cap)
    return pl.pallas_call(
        _norm_kernel,
        grid=(T // tm,),
        in_specs=[pl.BlockSpec((tm, D), lambda i: (i, 0)), pl.BlockSpec((1, D), lambda i: (0, 0))],
        out_specs=pl.BlockSpec((tm, D), lambda i: (i, 0)),
        out_shape=jax.ShapeDtypeStruct((T, D), BF16),
        compiler_params=_cparams("parallel"),
        name="norm",
    )(x, nw.reshape(1, D))


def _embed_kernel(x_ref, meta_ref, nw_ref, xp_ref, xn_ref, *, n_meta, rc):
    nw = nw_ref[...]
    meta = meta_ref[...]
    xp_ref[0, 0:n_meta, :] = meta
    xn_ref[0, 0:n_meta, :] = _rmsnorm_bf16(meta, nw)

    def body(c, carry):
        r = pl.multiple_of(c * rc, rc)
        ro = pl.multiple_of(n_meta + c * rc, 16)
        xf = x_ref[0, pl.ds(r, rc), :]
        xp_ref[0, pl.ds(ro, rc), :] = xf
        xn_ref[0, pl.ds(ro, rc), :] = _rmsnorm_bf16(xf, nw)
        return carry

    lax.fori_loop(0, x_ref.shape[1] // rc, body, 0)


def _embed(x, meta, nw):
    B, S, D = x.shape
    n_meta = meta.shape[0]
    L = n_meta + S
    rc = _row_tile(S, 128)
    full = pl.BlockSpec((1, L, D), lambda b: (b, 0, 0))
    return pl.pallas_call(
        functools.partial(_embed_kernel, n_meta=n_meta, rc=rc),
        grid=(B,),
        in_specs=[pl.BlockSpec((1, S, D), lambda b: (b, 0, 0)),
                  pl.BlockSpec((n_meta, D), lambda b: (0, 0)),
                  pl.BlockSpec((1, D), lambda b: (0, 0))],
        out_specs=[full, full],
        out_shape=[jax.ShapeDtypeStruct((B, L, D), F32), jax.ShapeDtypeStruct((B, L, D), BF16)],
        compiler_params=_cparams("parallel"),
        name="embed",
    )(x, meta, nw.reshape(1, D))


def _inproj_kernel(*refs, has_small):
    if has_small:
        x_ref, w_ref, ws_ref, o_ref, os_ref = refs

        @pl.when(pl.program_id(1) == 0)
        def _():
            os_ref[...] = _dot(x_ref[...], ws_ref[...].astype(BF16))
    else:
        x_ref, w_ref, o_ref = refs
    o_ref[...] = _dot(x_ref[...], w_ref[...].astype(BF16))


def _inproj(xn, w_all, layer, n_main, w_small_bf=None, tn=512, tm_cap=2064):
    T, D = xn.shape
    tm = _row_tile(T, tm_cap)
    has_small = w_small_bf is not None
    in_specs = [pl.BlockSpec((tm, D), lambda i, j: (i, 0)),
                pl.BlockSpec((None, D, tn), lambda i, j: (layer, 0, j))]
    out_shape = [jax.ShapeDtypeStruct((T, n_main), F32)]
    out_specs = [pl.BlockSpec((tm, tn), lambda i, j: (i, j))]
    args = [xn, w_all]
    if has_small:
        ns = w_small_bf.shape[1]
        in_specs.append(pl.BlockSpec((D, ns), lambda i, j: (0, 0)))
        out_shape.append(jax.ShapeDtypeStruct((T, ns), F32))
        out_specs.append(pl.BlockSpec((tm, ns), lambda i, j: (i, 0)))
        args.append(w_small_bf)
    out = pl.pallas_call(
        functools.partial(_inproj_kernel, has_small=has_small),
        grid=(T // tm, n_main // tn),
        in_specs=in_specs, out_specs=out_specs, out_shape=out_shape,
        compiler_params=_cparams("parallel", "arbitrary"),
        name="inproj",
    )(*args)
    return out if has_small else out[0]


def _outproj_kernel(*refs, n_parts):
    lhs_refs = refs[:n_parts]
    w_ref, x_ref, nw_ref, o_ref, on_ref = refs[n_parts:]
    acc = x_ref[...]
    k0 = 0
    for lr in lhs_refs:
        kk = lr.shape[1]
        acc = acc + _dot(lr[...], w_ref[k0:k0 + kk, :])
        k0 += kk
    o_ref[...] = acc
    on_ref[...] = _rmsnorm_bf16(acc, nw_ref[...])


def _outproj(lhs_parts, w_bf, x, nw_next, tm_cap=384):
    T, D = x.shape
    tm = _row_tile(T, tm_cap)
    row = pl.BlockSpec((tm, D), lambda i: (i, 0))
    in_specs = [pl.BlockSpec((tm, p.shape[1]), lambda i: (i, 0)) for p in lhs_parts]
    in_specs += [pl.BlockSpec(w_bf.shape, lambda i: (0, 0)), row, pl.BlockSpec((1, D), lambda i: (0, 0))]
    return pl.pallas_call(
        functools.partial(_outproj_kernel, n_parts=len(lhs_parts)),
        grid=(T // tm,),
        in_specs=in_specs,
        out_specs=[row, row],
        out_shape=[jax.ShapeDtypeStruct((T, D), F32), jax.ShapeDtypeStruct((T, D), BF16)],
        compiler_params=_cparams("parallel"),
        name="outproj",
    )(*lhs_parts, w_bf, x, nw_next.reshape(1, D))


def _final_norm_kernel(x_ref, w_ref, o_ref, *, n_skip, tr):
    t = pl.program_id(1)
    r = pl.multiple_of(n_skip + t * tr, 8)
    xf = x_ref[0, pl.ds(r, tr), :]
    ms = jnp.mean(xf * xf, axis=-1, keepdims=True)
    o_ref[0] = xf * lax.rsqrt(ms + EPS) * w_ref[...]


def _final_norm(x, w, n_skip):
    B, L, D = x.shape
    Lo = L - n_skip
    tr = _row_tile(Lo, 512, 8)
    return pl.pallas_call(
        functools.partial(_final_norm_kernel, n_skip=n_skip, tr=tr),
        grid=(B, Lo // tr),
        in_specs=[pl.BlockSpec((1, L, D), lambda b, t: (b, 0, 0)),
                  pl.BlockSpec((1, D), lambda b, t: (0, 0))],
        out_specs=pl.BlockSpec((1, tr, D), lambda b, t: (b, t, 0)),
        out_shape=jax.ShapeDtypeStruct((B, Lo, D), F32),
        compiler_params=_cparams("parallel", "arbitrary"),
        name="final_norm",
    )(x, w.reshape(1, D))


def _conv_rows(win, cw, n):
    kw = cw.shape[0]
    out = win[8:8 + n] * cw[kw - 1:kw, :]
    for tau in range(1, kw):
        out = out + pltpu.roll(win, tau, 0)[8:8 + n] * cw[kw - 1 - tau:kw - tau, :]
    return out


def _conv_at(ref, cw, r0, n, lanes, scr):
    kw = cw.shape[0]
    scr[...] = ref[0, pl.ds(pl.multiple_of(r0 - 8, 8), n + 8), lanes]
    out = scr[8:8 + n, :] * cw[kw - 1:kw, :]
    for tau in range(1, kw):
        out = out + scr[8 - tau:8 - tau + n, :] * cw[kw - 1 - tau:kw - tau, :]
    return out


def _gdn_prompt_kernel(alog_ref, dtb_ref, q_ref, k_ref, v_ref, cwq_ref, cwk_ref, cwv_ref,
                       braw_ref, araw_ref, gnw_ref, z_ref, o_ref, s_ref, S_scr, win_scr,
                       *, n_meta, n_chunks, dk, hg):
    h0 = pl.program_id(1) * hg
    C = GDN_CHUNK
    a_coefs = [-jnp.exp(jnp.full((1, 1), alog_ref[h0 + hh], F32)) for hh in range(hg)]
    dt_bs = [dtb_ref[h0 + hh] for hh in range(hg)]
    ii = lax.broadcasted_iota(jnp.int32, (C, C), 0)
    jj = lax.broadcasted_iota(jnp.int32, (C, C), 1)
    eye = (ii == jj).astype(F32)
    gnw = gnw_ref[...]

    def conv(ref, cw_ref, r0, first, lanes, slot):
        if first:
            win = jnp.concatenate([jnp.zeros((8, dk), F32), ref[0, 0:C, lanes]], axis=0)
            return _conv_rows(win, cw_ref[:, lanes], C)
        return _conv_at(ref, cw_ref[:, lanes], r0, C, lanes, win_scr.at[slot])

    def solve_stage(probs):
        qs, ks, vs, gates = [], [], [], []
        for p, (hh, r0, crow, first) in enumerate(probs):
            lanes = slice(hh * dk, (hh + 1) * dk)
            q = _silu(conv(q_ref, cwq_ref, r0, first, lanes, 3 * p))
            k = _silu(conv(k_ref, cwk_ref, r0, first, lanes, 3 * p + 1))
            v = _silu(conv(v_ref, cwv_ref, r0, first, lanes, 3 * p + 2))
            qs.append(q * (lax.rsqrt(jnp.sum(q * q, axis=-1, keepdims=True) + EPS) * (dk ** -0.5)))
            ks.append(k * lax.rsqrt(jnp.sum(k * k, axis=-1, keepdims=True) + EPS))
            vs.append(v)
            beta_row = _sigmoid(braw_ref[0, hh, pl.ds(crow, 1), :])
            g_row = a_coefs[hh] * _softplus(araw_ref[0, hh, pl.ds(crow, 1), :] + dt_bs[hh])
            if first:
                valid = lax.broadcasted_iota(jnp.int32, (1, C), 1) < n_meta
                beta_row = jnp.where(valid, beta_row, 0.0)
                g_row = jnp.where(valid, g_row, 0.0)
            g_rb = jnp.broadcast_to(g_row, (C, C))
            gc_col = jnp.sum(jnp.where(jj <= ii, g_rb, 0.0), axis=1, keepdims=True)
            g_col = jnp.sum(jnp.where(jj == ii, g_rb, 0.0), axis=1, keepdims=True)
            gc_row = jnp.sum(jnp.where(ii <= jj, jnp.broadcast_to(g_col, (C, C)), 0.0),
                             axis=0, keepdims=True)
            beta_col = jnp.sum(jnp.where(jj == ii, jnp.broadcast_to(beta_row, (C, C)), 0.0),
                               axis=1, keepdims=True)
            g_last = jnp.sum(g_row, axis=1, keepdims=True)
            decay = jnp.exp(jnp.where(ii >= jj, gc_col - gc_row, NEG_BIG))
            gates.append((beta_col, gc_col, g_last, decay, jnp.exp(gc_col)))
        n = len(probs)
        kbs = [ks[p] * gates[p][0] for p in range(n)]
        Xs = [_dot_nt(jnp.concatenate([qs[p], kbs[p]], axis=0).astype(BF16), ks[p].astype(BF16))
              for p in range(n)]
        qks = [Xs[p][:C] * gates[p][3] for p in range(n)]
        Ns = [-jnp.where(ii > jj, Xs[p][C:] * gates[p][3], 0.0) for p in range(n)]
        Tinvs = [eye + N for N in Ns]
        for _ in range(max(1, int(math.ceil(math.log2(C))) - 1)):
            Ns = [_dot(N.astype(BF16), N.astype(BF16)) for N in Ns]
            Tinvs = [Tinvs[p] + _dot(Tinvs[p].astype(BF16), Ns[p].astype(BF16)) for p in range(n)]
        sols = [_dot(Tinvs[p].astype(BF16),
                     jnp.concatenate([vs[p] * gates[p][0], kbs[p] * gates[p][4]], axis=1).astype(BF16))
                for p in range(n)]
        dv = vs[0].shape[1]
        return [(sols[p][:, :dv], sols[p][:, dv:], qks[p], qs[p] * gates[p][4],
                 ks[p] * jnp.exp(gates[p][2] - gates[p][1]), jnp.exp(gates[p][2])) for p in range(n)]

    def state_stage(items, r0, n_store):
        Ss = [S_scr[hh] for hh in range(hg)]
        WQs = [_dot(jnp.concatenate([items[hh][1], items[hh][3]], axis=0).astype(BF16), Ss[hh].astype(BF16))
               for hh in range(hg)]
        v_news = [items[hh][0] - WQs[hh][:C] for hh in range(hg)]
        os_ = [WQs[hh][C:] + _dot(items[hh][2].astype(BF16), v_news[hh].astype(BF16)) for hh in range(hg)]
        S_news = [Ss[hh] * items[hh][5] + _dot_tn(items[hh][4].astype(BF16), v_news[hh].astype(BF16))
                  for hh in range(hg)]
        for hh in range(hg):
            lanes = slice(hh * dk, (hh + 1) * dk)
            S_scr[hh] = S_news[hh]
            o = os_[hh]
            on = o * lax.rsqrt(jnp.mean(o * o, axis=-1, keepdims=True) + EPS) * gnw
            o_ref[0, pl.ds(r0, n_store), lanes] = (
                on[:n_store] * _silu(z_ref[0, pl.ds(r0, n_store), lanes])).astype(o_ref.dtype)

    S_scr[...] = jnp.zeros(S_scr.shape, F32)
    state_stage(solve_stage([(hh, 0, 0, True) for hh in range(hg)]), 0, n_meta)

    cb = GDN_CHUNKS_PER_ITER if n_chunks % GDN_CHUNKS_PER_ITER == 0 else 1

    def body(t, carry):
        r0s = [pl.multiple_of(n_meta + (t * cb + cc) * C, 16) for cc in range(cb)]
        items = solve_stage([(hh, r0s[cc], t * cb + cc + 1, False) for cc in range(cb) for hh in range(hg)])
        for cc in range(cb):
            state_stage(items[cc * hg:(cc + 1) * hg], r0s[cc], C)
        return carry

    lax.fori_loop(0, n_chunks // cb, body, 0)
    s_ref[0] = S_scr[...]


def _gdn_prompt(P, gates_c, conv_w, a_log, dt_bias, gn_w, n_meta, H, dk, dv, off_z):
    B, L, _ = P.shape
    n_chunks = (L - n_meta) // GDN_CHUNK
    hg = GDN_HEADS_PER_STEP if H % GDN_HEADS_PER_STEP == 0 else 1
    ng = H // hg
    zb0 = off_z // (hg * dv)
    cw = conv_w.shape[0]
    spec_col = lambda off: pl.BlockSpec((1, L, hg * dk), lambda b, g: (b, 0, off + g))
    spec_cw = lambda off: pl.BlockSpec((cw, hg * dk), lambda b, g: (0, off + g))
    smem = pl.BlockSpec(memory_space=pltpu.SMEM)
    nc1 = gates_c.shape[2]
    return pl.pallas_call(
        functools.partial(_gdn_prompt_kernel, n_meta=n_meta, n_chunks=n_chunks, dk=dk, hg=hg),
        grid=(B, ng),
        in_specs=[smem, smem, spec_col(0), spec_col(ng), spec_col(2 * ng),
                  spec_cw(0), spec_cw(ng), spec_cw(2 * ng),
                  pl.BlockSpec((1, hg, nc1, GDN_CHUNK), lambda b, g: (b, g, 0, 0)),
                  pl.BlockSpec((1, hg, nc1, GDN_CHUNK), lambda b, g: (b, ng + g, 0, 0)),
                  pl.BlockSpec((1, dv), lambda b, g: (0, 0)),
                  pl.BlockSpec((1, L, hg * dv), lambda b, g: (b, 0, zb0 + g))],
        out_specs=[pl.BlockSpec((1, L, hg * dv), lambda b, g: (b, 0, g)),
                   pl.BlockSpec((1, hg, dk, dv), lambda b, g: (b, g, 0, 0))],
        out_shape=[jax.ShapeDtypeStruct((B, L, H * dv), BF16),
                   jax.ShapeDtypeStruct((B, H, dk, dv), F32)],
        scratch_shapes=[pltpu.VMEM((hg, dk, dv), F32),
                        pltpu.VMEM((3 * hg * GDN_CHUNKS_PER_ITER, GDN_CHUNK + 8, dk), F32)],
        compiler_params=_cparams("parallel", "arbitrary"),
        name="gdn_prompt",
    )(a_log, dt_bias, P, P, P, conv_w, conv_w, conv_w, gates_c, gates_c,
      gn_w.reshape(1, dv), P)


def _sb_prompt_kernel(*refs, layer, n_fill, **kw):
    ins = refs[:5]
    o_ref, ko_ref, vo_ref = refs[-3:]
    if n_fill > 1:
        for other in range(n_fill):
            if other != layer:
                ko_ref[other] = jnp.zeros(ko_ref.shape[1:], F32)
                vo_ref[other] = jnp.zeros(vo_ref.shape[1:], F32)
        ko_ref, vo_ref = ko_ref.at[layer], vo_ref.at[layer]
    _sb_prompt_body(pl.program_id(1), *ins, o_ref, ko_ref, vo_ref, **kw)


def _sb_prompt_body(h, bias_ref, q_ref, k_ref, v_ref, z_ref, o_ref, ko_ref, vo_ref,
                    *, n_meta, n_qb, qb, dh):
    MB = SB_BLOCK
    bias = bias_ref[h]
    scale = dh ** -0.5

    def strict_lower(n):
        return jnp.where(lax.broadcasted_iota(jnp.int32, (n, n), 0)
                         > lax.broadcasted_iota(jnp.int32, (n, n), 1), -1.0, 0.0).astype(BF16)

    gw_main = min(qb, SB_GROUP)
    U_main = strict_lower(gw_main)
    U_meta = U_main if gw_main == MB else strict_lower(MB)

    ko_ref[...] = k_ref[...]
    vo_ref[...] = v_ref[...]

    def tile(q_bf, ks, kw, mode, carry):
        acc, c = carry
        rows = q_bf.shape[0]
        kblk = k_ref[0, pl.ds(ks, kw), :].astype(BF16)
        vblk = v_ref[0, pl.ds(ks, kw), :].astype(BF16)
        z = _dot_nt(q_bf, kblk) + bias
        sp = jnp.maximum(z, 0.0) + jnp.log(1.0 + jnp.exp(-jnp.abs(z)))
        ls = z - sp
        if mode is None:
            vis = None
        else:
            col = lax.broadcasted_iota(jnp.int32, (rows, kw), 1)
            vis = (col < lax.broadcasted_iota(jnp.int32, (rows, kw), 0)) if mode == "diag" else (col < n_meta)
            sp = jnp.where(vis, sp, 0.0)
        gw = min(kw, SB_GROUP)
        U = U_main if gw == gw_main else U_meta
        sp_bf = sp.astype(BF16)
        a_parts = [None] * (kw // gw)
        for g in range(kw // gw - 1, -1, -1):
            sl = slice(g * gw, (g + 1) * gw)
            E = _dot(sp_bf[:, sl], U)
            a = jnp.exp(ls[:, sl] + E + c)
            a_parts[g] = a if vis is None else jnp.where(vis[:, sl], a, 0.0)
            c = c + (E[:, 0:1] - sp[:, g * gw:g * gw + 1])
        a_all = a_parts[0] if len(a_parts) == 1 else jnp.concatenate(a_parts, axis=1)
        return acc + _dot(a_all.astype(BF16), vblk), c

    def zero(rows):
        return (jnp.zeros((rows, dh), F32), jnp.zeros((rows, 1), F32))

    acc0, _ = tile((q_ref[0, 0:MB, :] * scale).astype(BF16), 0, MB, "diag", zero(MB))
    o_ref[0, 0:n_meta, :] = (acc0[:n_meta] * _silu(z_ref[0, 0:n_meta, :])).astype(o_ref.dtype)

    def qblock(i, carry):
        qs = pl.multiple_of(n_meta + i * qb, 16)
        q_bf = (q_ref[0, pl.ds(qs, qb), :] * scale).astype(BF16)
        st = tile(q_bf, qs, qb, "diag", zero(qb))

        def kstep(t, st):
            ks = pl.multiple_of(n_meta + (i - 1 - t) * qb, 16)
            return tile(q_bf, ks, qb, None, st)

        st = lax.fori_loop(0, i, kstep, st)
        acc, _ = tile(q_bf, 0, MB, "meta", st)
        o_ref[0, pl.ds(qs, qb), :] = (acc * _silu(z_ref[0, pl.ds(qs, qb), :])).astype(o_ref.dtype)
        return carry

    lax.fori_loop(0, n_qb, qblock, 0)


def _sb_prompt(P, sb_bias, n_meta, H, dh, off_q, off_z, layer, n_layers, kv_slabs=None):
    B, L, _ = P.shape
    qb = next(c for c in (512, 256, 128) if (L - n_meta) % c == 0)
    n_qb = (L - n_meta) // qb
    qb0, zb0 = off_q // dh, off_z // dh
    n_fill = n_layers if kv_slabs is None else 1
    col = lambda off: pl.BlockSpec((1, L, dh), lambda b, h: (b, 0, off + h))
    if kv_slabs is None:
        slab = pl.BlockSpec((n_layers, 1, L, dh), lambda b, h: (0, b, 0, h))
    else:
        slab = pl.BlockSpec((None, 1, L, dh), lambda b, h: (layer, b, 0, h))
    in_specs = [pl.BlockSpec(memory_space=pltpu.SMEM), col(qb0), col(qb0 + H), col(qb0 + 2 * H), col(zb0)]
    args = [sb_bias, P, P, P, P]
    aliases = {}
    if kv_slabs is not None:
        in_specs += [pl.BlockSpec(memory_space=pl.ANY)] * 2
        aliases = {len(args): 1, len(args) + 1: 2}
        args += list(kv_slabs)
    slab_shape = jax.ShapeDtypeStruct((n_layers, B, L, H * dh), F32)
    return pl.pallas_call(
        functools.partial(_sb_prompt_kernel, layer=layer, n_fill=n_fill, n_meta=n_meta, n_qb=n_qb, qb=qb, dh=dh),
        grid=(B, H),
        in_specs=in_specs,
        out_specs=[col(0), slab, slab],
        out_shape=[jax.ShapeDtypeStruct((B, L, H * dh), BF16), slab_shape, slab_shape],
        input_output_aliases=aliases,
        compiler_params=_cparams("parallel", "arbitrary"),
        name="sb_prompt",
    )(*args)


def _lru_gates(xc, w_ref, ba, bx, ls_lam):
    bw = xc.shape[1]
    pre = _dot(xc.astype(BF16), w_ref[0])
    r = _sigmoid(pre[:, :bw] + ba)
    i = _sigmoid(pre[:, bw:] + bx)
    log_a = RG_C * r * ls_lam
    a = jnp.exp(log_a)
    om = 1.0 - a * a
    root = jnp.where(om > 0.0, om * lax.rsqrt(om), 0.0)
    return a, root * (i * xc)


def _lru_prompt_kernel(x_ref, g_ref, cw_ref, cb_ref, w_ref, ba_ref, bx_ref, lam_ref,
                       y_ref, hl_ref, h_scr, win_scr, ab_scr, hin_scr, *, n_meta, n_chunks):
    TC = LRU_CHUNK
    cw = cw_ref[...]
    cb = cb_ref[...]
    ba, bx = ba_ref[...], bx_ref[...]
    ls_lam = -_softplus(-lam_ref[...])
    bw = x_ref.shape[2]

    def scan_chunk(xc, n, r0, h_prev):
        a, b = _lru_gates(xc + cb, w_ref, ba, bx, ls_lam)
        k = 1
        while k < n:
            a_s = _shift_down(a, k, 1.0)
            b_s = _shift_down(b, k, 0.0)
            b = a * b_s + b
            a = a * a_s
            k *= 2
        hrows = a * h_prev + b
        y_ref[0, pl.ds(r0, n), :] = (hrows * _silu(g_ref[0, pl.ds(r0, n), :])).astype(y_ref.dtype)
        return hrows[n - 1:n, :]

    def scan_prep(r0, slot):
        xc = _conv_at(x_ref, cw, r0, TC, slice(None), win_scr.at[slot])
        a, b = _lru_gates(xc + cb, w_ref, ba, bx, ls_lam)
        sub = lax.broadcasted_iota(jnp.int32, (TC, bw), 0) % 8
        for k in (1, 2, 4):
            keep = sub >= k
            a_s = jnp.where(keep, pltpu.roll(a, k, 0), 1.0)
            b_s = jnp.where(keep, pltpu.roll(b, k, 0), 0.0)
            b = a * b_s + b
            a = a * a_s
        ng = TC // 8
        ab_scr[slot, 0] = a
        ab_scr[slot, 1] = b
        at = ab_scr[slot, 0, pl.ds(7, ng, stride=8), :]
        bt = ab_scr[slot, 1, pl.ds(7, ng, stride=8), :]
        k = 1
        while k < ng:
            at_s = _shift_down(at, k, 1.0)
            bt_s = _shift_down(bt, k, 0.0)
            bt = at * bt_s + bt
            at = at * at_s
            k *= 2
        return a, b, _shift_down(at, 1, 1.0), _shift_down(bt, 1, 0.0)

    def scan_finish(prep, r0, slot, h_prev):
        a, b, atx, btx = prep
        ng = TC // 8
        hin_scr[slot] = atx * h_prev + btx
        hrows = jnp.concatenate(
            [a[8 * g:8 * g + 8] * hin_scr[slot, g:g + 1, :] + b[8 * g:8 * g + 8] for g in range(ng)], axis=0)
        y_ref[0, pl.ds(r0, TC), :] = (hrows * _silu(g_ref[0, pl.ds(r0, TC), :])).astype(y_ref.dtype)
        return hrows[TC - 1:TC, :]

    win0 = jnp.concatenate([jnp.zeros((8, bw), F32), x_ref[0, 0:n_meta, :]], axis=0)
    h_scr[...] = scan_chunk(_conv_rows(win0, cw, n_meta), n_meta, 0, jnp.zeros((1, bw), F32))

    cb_n = LRU_CHUNKS_PER_ITER if n_chunks % LRU_CHUNKS_PER_ITER == 0 else 1

    def body(t, carry):
        r0s = [pl.multiple_of(n_meta + (t * cb_n + cc) * TC, 16) for cc in range(cb_n)]
        preps = [scan_prep(r0s[cc], cc) for cc in range(cb_n)]
        h = h_scr[...]
        for cc in range(cb_n):
            h = scan_finish(preps[cc], r0s[cc], cc, h)
        h_scr[...] = h
        return carry

    lax.fori_loop(0, n_chunks // cb_n, body, 0)
    hl_ref[0] = h_scr[...]


def _lru_prompt(P, conv_w, conv_b, w_ax_bf, b_a, b_x, lam, n_meta, w_c):
    B, L, _ = P.shape
    hc, bw, _ = w_ax_bf.shape
    n_chunks = (L - n_meta) // LRU_CHUNK
    kw = conv_w.shape[0]
    vec = lambda a: a.reshape(1, w_c)
    vspec = pl.BlockSpec((1, bw), lambda b, h: (0, h))
    return pl.pallas_call(
        functools.partial(_lru_prompt_kernel, n_meta=n_meta, n_chunks=n_chunks),
        grid=(B, hc),
        in_specs=[pl.BlockSpec((1, L, bw), lambda b, h: (b, 0, h)),
                  pl.BlockSpec((1, L, bw), lambda b, h: (b, 0, hc + h)),
                  pl.BlockSpec((kw, bw), lambda b, h: (0, h)),
                  vspec,
                  pl.BlockSpec((1, bw, 2 * bw), lambda b, h: (h, 0, 0)),
                  vspec, vspec, vspec],
        out_specs=[pl.BlockSpec((1, L, bw), lambda b, h: (b, 0, h)),
                   pl.BlockSpec((1, 1, bw), lambda b, h: (b, 0, h))],
        out_shape=[jax.ShapeDtypeStruct((B, L, w_c), BF16),
                   jax.ShapeDtypeStruct((B, 1, w_c), F32)],
        scratch_shapes=[pltpu.VMEM((1, bw), F32),
                        pltpu.VMEM((LRU_CHUNKS_PER_ITER, LRU_CHUNK + 8, bw), F32),
                        pltpu.VMEM((LRU_CHUNKS_PER_ITER, 2, LRU_CHUNK, bw), F32),
                        pltpu.VMEM((LRU_CHUNKS_PER_ITER, LRU_CHUNK // 8, bw), F32)],
        compiler_params=_cparams("parallel", "arbitrary"),
        name="lru_prompt",
    )(P, P, conv_w, vec(conv_b), w_ax_bf, vec(b_a), vec(b_x), vec(lam))


def _gdn_sample_kernel(alog_ref, dtb_ref, u_ref, buf_ref, cw_ref, gates_ref, gnw_ref, z_ref, s_ref,
                       o_ref, so_ref, *, H, dk, dv):
    kw = cw_ref.shape[0]
    xc = u_ref[0] * cw_ref[kw - 1:kw, :]
    for t in range(kw - 1):
        xc = xc + buf_ref[0, t:t + 1, :] * cw_ref[t:t + 1, :]
    xc = _silu(xc)
    gates = gates_ref[0]
    gnw = gnw_ref[...]
    row8 = lax.broadcasted_iota(jnp.int32, (8, dk), 0)
    di = lax.broadcasted_iota(jnp.int32, (dk, dk), 0)
    dj = lax.broadcasted_iota(jnp.int32, (dk, dk), 1)
    for h in range(H):
        q = xc[:, h * dk:(h + 1) * dk]
        k = xc[:, (H + h) * dk:(H + h + 1) * dk]
        v = xc[:, 2 * H * dk + h * dv:2 * H * dk + (h + 1) * dv]
        q = q * (lax.rsqrt(jnp.sum(q * q, axis=-1, keepdims=True) + EPS) * (dk ** -0.5))
        k = k * lax.rsqrt(jnp.sum(k * k, axis=-1, keepdims=True) + EPS)
        beta = _sigmoid(gates[:, h:h + 1])
        g = -jnp.exp(jnp.full((1, 1), alog_ref[h], F32)) * _softplus(gates[:, H + h:H + h + 1] + dtb_ref[h])
        eg = jnp.exp(g)
        S = s_ref[0, h]
        kq = jnp.where(row8 == 0, jnp.broadcast_to(k, (8, dk)),
                       jnp.where(row8 == 1, jnp.broadcast_to(q * eg, (8, dk)), 0.0))
        R = _dot(kq.astype(BF16), S.astype(BF16))
        v_new = beta * (v - eg * R[0:1])
        o = R[1:2] + jnp.sum(q * k, axis=-1, keepdims=True) * v_new
        k_col = jnp.sum(jnp.where(di == dj, jnp.broadcast_to(k, (dk, dk)), 0.0),
                        axis=1, keepdims=True)
        so_ref[0, h] = S * eg + k_col * v_new
        on = o * lax.rsqrt(jnp.mean(o * o, axis=-1, keepdims=True) + EPS) * gnw
        o_ref[0, :, h * dv:(h + 1) * dv] = (on * _silu(z_ref[0, :, h * dv:(h + 1) * dv])).astype(o_ref.dtype)


def _gdn_sample(u, buf, conv_w, gates, a_log, dt_bias, gn_w, z_a, S, layer, H, dk, dv):
    Bd, C = u.shape
    kw = conv_w.shape[0]
    smem = pl.BlockSpec(memory_space=pltpu.SMEM)
    row = lambda n: pl.BlockSpec((1, 1, n), lambda b: (b, 0, 0))
    o, s_new = pl.pallas_call(
        functools.partial(_gdn_sample_kernel, H=H, dk=dk, dv=dv),
        grid=(Bd,),
        in_specs=[smem, smem, row(C),
                  pl.BlockSpec((1, kw - 1, C), lambda b: (b, 0, 0)),
                  pl.BlockSpec((kw, C), lambda b: (0, 0)),
                  row(2 * H),
                  pl.BlockSpec((1, dv), lambda b: (0, 0)),
                  row(H * dv),
                  pl.BlockSpec((None, 1, H, dk, dv), lambda b: (layer, b, 0, 0, 0))],
        out_specs=[row(H * dv), pl.BlockSpec((1, H, dk, dv), lambda b: (b, 0, 0, 0))],
        out_shape=[jax.ShapeDtypeStruct((Bd, 1, H * dv), BF16),
                   jax.ShapeDtypeStruct((Bd, H, dk, dv), F32)],
        compiler_params=_cparams("parallel"),
        name="gdn_sample",
    )(a_log, dt_bias, u.reshape(Bd, 1, C), buf, conv_w, gates.reshape(Bd, 1, 2 * H),
      gn_w.reshape(1, dv), z_a.reshape(Bd, 1, H * dv), S)
    return o.reshape(Bd, H * dv), s_new


def _sb_sample_kernel(pt_ref, q_ref, bias_ref, z_ref, *refs, H, dh, page, pps, n_steps):
    k_refs = refs[:pps]
    v_refs = refs[pps:2 * pps]
    o_ref, acc_scr, c_scr = refs[2 * pps:]
    p = pl.program_id(1)
    LW = 128
    nv = (page * H) // LW

    @pl.when(p == 0)
    def _():
        acc_scr[...] = jnp.zeros(acc_scr.shape, F32)
        c_scr[...] = jnp.zeros(c_scr.shape, F32)

    q_bf = (q_ref[0] * (dh ** -0.5)).astype(BF16)
    bias = bias_ref[...]
    lane = lax.broadcasted_iota(jnp.int32, (H, LW), 1)
    sub = lax.broadcasted_iota(jnp.int32, (H, LW), 0)
    valid = (lane % H) == sub
    ui = lax.broadcasted_iota(jnp.int32, (LW, LW), 0)
    uj = lax.broadcasted_iota(jnp.int32, (LW, LW), 1)
    U = (ui > uj).astype(BF16)

    ls_all, l1m_all = [], []
    for s in range(pps):
        kp = k_refs[s][0, 0].astype(BF16)
        z = _dot_nt(q_bf, kp) + bias
        sp = jnp.maximum(z, 0.0) + jnp.log(1.0 + jnp.exp(-jnp.abs(z)))
        ls_all.append(z - sp)
        l1m_all += [jnp.where(valid, -sp[:, i * LW:(i + 1) * LW], 0.0) for i in range(nv)]
    L = jnp.concatenate(l1m_all, axis=0)
    hi = L.astype(BF16)
    lo = (L - hi.astype(F32)).astype(BF16)
    n_l = pps * nv * H
    E2 = _dot(jnp.concatenate([hi, lo], axis=0), U)
    E = E2[:n_l] + E2[n_l:]
    tot = E[:, 0:1] + L[:, 0:1]
    acc = acc_scr[...]
    c = c_scr[...]
    for s in range(pps):
        a_parts = [None] * nv
        for i in range(nv - 1, -1, -1):
            g = s * nv + i
            loga = ls_all[s][:, i * LW:(i + 1) * LW] + E[g * H:(g + 1) * H] + c
            a_parts[i] = jnp.where(valid, jnp.exp(loga), 0.0)
            c = c + tot[g * H:(g + 1) * H]
        a = jnp.concatenate(a_parts, axis=1).astype(BF16)
        acc = acc + _dot(a, v_refs[s][0, 0].astype(BF16))
    acc_scr[...] = acc
    c_scr[...] = c

    @pl.when(p == n_steps - 1)
    def _():
        o_ref[0] = (acc * _silu(z_ref[0])).astype(o_ref.dtype)


def _sb_sample(q, z_b, sb_bias, cache_k, cache_v, page_table, layer):
    Bd, H, dh = q.shape
    _, n_phys, page, _, _ = cache_k.shape
    n_pages = page_table.shape[1]
    pps = SB_PAGES_PER_STEP if n_pages % SB_PAGES_PER_STEP == 0 else 1
    n_steps = n_pages // pps
    ck = cache_k.reshape(cache_k.shape[0], n_phys, page * H, dh)
    cv = cache_v.reshape(cache_v.shape[0], n_phys, page * H, dh)

    def page_spec(s):
        def imap(b, p, pt):
            return (layer, pt[b * n_pages + (n_pages - 1 - (p * pps + s))], 0, 0)
        return pl.BlockSpec((1, 1, page * H, dh), imap)

    hd = pl.BlockSpec((1, H, dh), lambda b, p, pt: (b, 0, 0))
    grid_spec = pltpu.PrefetchScalarGridSpec(
        num_scalar_prefetch=1,
        grid=(Bd, n_steps),
        in_specs=[hd, pl.BlockSpec((H, 1), lambda b, p, pt: (0, 0)), hd]
                 + [page_spec(s) for s in range(pps)] * 2,
        out_specs=hd,
        scratch_shapes=[pltpu.VMEM((H, dh), F32), pltpu.VMEM((H, 1), F32)],
    )
    return pl.pallas_call(
        functools.partial(_sb_sample_kernel, H=H, dh=dh, page=page, pps=pps, n_steps=n_steps),
        grid_spec=grid_spec,
        out_shape=jax.ShapeDtypeStruct((Bd, H, dh), BF16),
        compiler_params=_cparams("parallel", "arbitrary"),
        name="sb_sample",
    )(page_table.reshape(-1), q, sb_bias.reshape(H, 1), z_b, *([ck] * pps), *([cv] * pps))


def _lru_sample_kernel(x_ref, g_ref, buf_ref, cw_ref, cb_ref, w_ref, ba_ref, bx_ref, lam_ref, h0_ref,
                       y_ref, h_ref):
    kw = cw_ref.shape[0]
    xc = x_ref[...] * cw_ref[kw - 1:kw, :] + cb_ref[...]
    for t in range(kw - 1):
        xc = xc + buf_ref[t] * cw_ref[t:t + 1, :]
    ls_lam = -_softplus(-lam_ref[...])
    a, b = _lru_gates(xc, w_ref, ba_ref[...], bx_ref[...], ls_lam)
    hn = a * h0_ref[...] + b
    h_ref[...] = hn
    y_ref[...] = (hn * _silu(g_ref[...])).astype(y_ref.dtype)


def _lru_sample(P, buf_t, conv_w, conv_b, w_ax_bf, b_a, b_x, lam, h0, w_c):
    Bd = P.shape[0]
    hc, bw, _ = w_ax_bf.shape
    kw = conv_w.shape[0]
    vec = lambda a: a.reshape(1, w_c)
    vspec = pl.BlockSpec((1, bw), lambda h: (0, h))
    mspec = lambda off: pl.BlockSpec((Bd, bw), lambda h: (0, off + h))
    return pl.pallas_call(
        _lru_sample_kernel,
        grid=(hc,),
        in_specs=[mspec(0), mspec(hc),
                  pl.BlockSpec((kw - 1, Bd, bw), lambda h: (0, 0, h)),
                  pl.BlockSpec((kw, bw), lambda h: (0, h)),
                  vspec,
                  pl.BlockSpec((1, bw, 2 * bw), lambda h: (h, 0, 0)),
                  vspec, vspec, vspec, mspec(0)],
        out_specs=[mspec(0), mspec(0)],
        out_shape=[jax.ShapeDtypeStruct((Bd, w_c), BF16), jax.ShapeDtypeStruct((Bd, w_c), F32)],
        compiler_params=_cparams("parallel"),
        name="lru_sample",
    )(P, P, buf_t, conv_w, vec(conv_b), w_ax_bf, vec(b_a), vec(b_x), vec(lam), h0)


def kernel(x_prompt, x_sample, cache_sb_k, cache_sb_v, state_gdn, state_gdn_conv, state_lru,
           state_lru_conv, page_table, meta_tokens, norm_w, final_norm_w, w_in_even, gdn_conv_w,
           gdn_a_log, gdn_dt_bias, gdn_norm_w, sb_bias, w_out_even, w_in_odd, lru_conv_w, lru_conv_b,
           lru_w_a, lru_b_a, lru_w_x, lru_b_x, lru_lambda, w_out_odd):
    B, seq, D = x_prompt.shape
    Bd = x_sample.shape[0]
    assert x_sample.shape[1] == 1
    n_meta = meta_tokens.shape[0]
    L = n_meta + seq
    depth = norm_w.shape[0]
    _, _, H_a, dk, dv = state_gdn.shape
    _, _, _, H_b, dh = cache_sb_k.shape
    w_a_tot, w_b_tot = H_a * dv, H_b * dh
    qkv_a = 2 * H_a * dk + H_a * dv
    off_qkv_b = qkv_a
    off_z_a = off_qkv_b + 3 * w_b_tot
    off_z_b = off_z_a + w_a_tot
    off_gate = off_z_b + w_b_tot
    w_c = state_lru.shape[2]
    assert dk == dv == dh == 128 and n_meta % 16 == 0 and n_meta <= GDN_CHUNK
    assert seq % LRU_CHUNK == 0 and seq % SB_BLOCK == 0 and seq % GDN_CHUNK == 0
    assert w_in_even.shape[2] == off_gate + 2 * H_a and lru_w_a.shape[2] == 128

    xp, xn_p = _embed(x_prompt, meta_tokens.astype(x_prompt.dtype), norm_w[0])
    xp, xn_p = xp.reshape(B * L, D), xn_p.reshape(B * L, D)
    xs = x_sample.reshape(Bd, D)
    xn_s = _norm(xs, norm_w[0])

    n_chunks = seq // GDN_CHUNK
    n_even = (depth + 1) // 2
    k_slab = v_slab = None
    pS, pcg, ph, pcl = [], [], [], []
    sk, sv, sS, scg, sh, scl = [], [], [], [], [], []
    for layer in range(depth):
        j = layer // 2
        nw_next = norm_w[layer + 1] if layer + 1 < depth else final_norm_w
        if layer % 2 == 0:
            w_gate = w_in_even[j, :, off_gate:]
            w_out_bf = w_out_even[j].astype(BF16)
            P, gates = _inproj(xn_p, w_in_even, j, off_gate, w_gate)
            P3 = P.reshape(B, L, off_gate)
            g3 = jnp.swapaxes(gates.reshape(B, L, 2 * H_a), 1, 2)
            g_meta = jnp.pad(g3[:, :, :n_meta], ((0, 0), (0, 0), (0, GDN_CHUNK - n_meta)))
            gates_c = jnp.concatenate(
                [g_meta[:, :, None, :], g3[:, :, n_meta:].reshape(B, 2 * H_a, n_chunks, GDN_CHUNK)], axis=2)
            la, S_p = _gdn_prompt(P3, gates_c, gdn_conv_w[j], gdn_a_log[j], gdn_dt_bias[j], gdn_norm_w[j],
                                  n_meta, H_a, dk, dv, off_z_a)
            lb, k_slab, v_slab = _sb_prompt(P3, sb_bias[j], n_meta, H_b, dh, off_qkv_b, off_z_b, j, n_even,
                                            None if j == 0 else (k_slab, v_slab))
            xp, xn_p = _outproj([la.reshape(B * L, w_a_tot), lb.reshape(B * L, w_b_tot)], w_out_bf, xp, nw_next)
            pS.append(S_p); pcg.append(P3[:, L - (gdn_conv_w.shape[1] - 1):, :qkv_a])
            Ps, gs = _inproj(xn_s, w_in_even, j, off_gate, w_gate)
            u = Ps[:, :qkv_a]
            las, S_s = _gdn_sample(u, state_gdn_conv[j], gdn_conv_w[j], gs, gdn_a_log[j], gdn_dt_bias[j],
                                   gdn_norm_w[j], Ps[:, off_z_a:off_z_a + w_a_tot], state_gdn, j, H_a, dk, dv)
            q_s = Ps[:, off_qkv_b:off_qkv_b + w_b_tot].reshape(Bd, H_b, dh)
            k_s = Ps[:, off_qkv_b + w_b_tot:off_qkv_b + 2 * w_b_tot]
            v_s = Ps[:, off_qkv_b + 2 * w_b_tot:off_qkv_b + 3 * w_b_tot]
            z_s = Ps[:, off_z_b:off_z_b + w_b_tot].reshape(Bd, H_b, dh)
            lbs = _sb_sample(q_s, z_s, sb_bias[j], cache_sb_k, cache_sb_v, page_table, j)
            xs, xn_s = _outproj([las, lbs.reshape(Bd, w_b_tot)], w_out_bf, xs, nw_next)
            sk.append(k_s.reshape(Bd, 1, H_b, dh)); sv.append(v_s.reshape(Bd, 1, H_b, dh))
            sS.append(S_s)
            scg.append(jnp.concatenate([state_gdn_conv[j][:, 1:], u[:, None, :]], axis=1))
        else:
            w_out_bf = w_out_odd[j].astype(BF16)
            w_ax_bf = jnp.concatenate([lru_w_a[j], lru_w_x[j]], axis=-1).astype(BF16)
            lru_args = (lru_conv_w[j], lru_conv_b[j], w_ax_bf, lru_b_a[j], lru_b_x[j], lru_lambda[j])
            P = _inproj(xn_p, w_in_odd, j, 2 * w_c)
            P3 = P.reshape(B, L, 2 * w_c)
            y, h_last = _lru_prompt(P3, *lru_args, n_meta, w_c)
            xp, xn_p = _outproj([y.reshape(B * L, w_c)], w_out_bf, xp, nw_next)
            ph.append(h_last.reshape(B, w_c)); pcl.append(P3[:, L - (lru_conv_w.shape[1] - 1):, :w_c])
            Ps = _inproj(xn_s, w_in_odd, j, 2 * w_c)
            ys, h_s = _lru_sample(Ps, jnp.swapaxes(state_lru_conv[j], 0, 1), *lru_args, state_lru[j], w_c)
            xs, xn_s = _outproj([ys], w_out_bf, xs, nw_next)
            sh.append(h_s)
            scl.append(jnp.concatenate([state_lru_conv[j][:, 1:], Ps[:, None, :w_c]], axis=1))

    y_prompt = _final_norm(xp.reshape(B, L, D), final_norm_w, n_meta)
    y_sample = _final_norm(xs.reshape(1, Bd, D), final_norm_w, 0).reshape(Bd, 1, D)
    st = jnp.stack
    kv5 = (n_even, B, L, H_b, dh)
    return (y_prompt, y_sample, k_slab.reshape(kv5), v_slab.reshape(kv5), st(pS), st(pcg), st(ph), st(pcl),
            st(sk), st(sv), st(sS), st(scg), st(sh), st(scl))
```
